```python
import math
import jax
import jax.numpy as jnp
from jax import lax
import numpy as np

D_MODEL = 1024
BATCH = 8
SEQ = 2048
DEPTH = 4

HEAD_DIM = 64
A_HEADS = 8
A_WIDTH = A_HEADS * HEAD_DIM
DILATED_PATTERNS = ((128, 1), (512, 4), (2048, 16))
DIL_BLOCK = 128
DIFF_HEADS = 4
DIFF_QK_WIDTH = 2 * DIFF_HEADS * HEAD_DIM
DIFF_V_DIM = 2 * HEAD_DIM
DIFF_V_WIDTH = DIFF_HEADS * DIFF_V_DIM
Q_BLOCK = 128
ATT_IN_WIDTH = 3 * A_WIDTH + 2 * DIFF_QK_WIDTH + DIFF_V_WIDTH
ATT_MIX_WIDTH = A_WIDTH + DIFF_V_WIDTH
ROPE_THETA = 10000.0
HGRN_EXPAND = 128
HGRN_HEADS = D_MODEL // HGRN_EXPAND
HGRN_KEY_DIM = HGRN_EXPAND
HGRN_VAL_DIM = D_MODEL // HGRN_HEADS
HGRN_FORGET_WIDTH = HGRN_HEADS * HGRN_KEY_DIM
HGRN_VAL_WIDTH = HGRN_HEADS * HGRN_VAL_DIM
REC_IN_WIDTH = 2 * HGRN_FORGET_WIDTH + 2 * HGRN_VAL_WIDTH
REC_MIX_WIDTH = HGRN_VAL_WIDTH
HGRN_CHUNK = 64
FFN_HIDDEN = (8 * D_MODEL + 3 * 256 - 1) // (3 * 256) * 256
N_ATT_LAYERS = (DEPTH + 1) // 2
N_REC_LAYERS = DEPTH // 2
NORM_EPS = 1e-6
NEG_INF = -1e30

kernel_name = 'hybrid_dilated_diff_hgrn2_trunk'


def rms_norm(x, gain):
    xf = x.astype(jnp.float32)
    y = xf * lax.rsqrt(jnp.mean(xf * xf, axis=-1, keepdims=True) + NORM_EPS)
    return (y * gain.astype(jnp.float32)).astype(x.dtype)


def rope_tables(seq_len, dim):
    inv_freq = ROPE_THETA ** (-jnp.arange(0, dim, 2, dtype=jnp.float32) / dim)
    ang = jnp.arange(seq_len, dtype=jnp.float32)[:, None] * inv_freq[None, :]
    return jnp.cos(ang), jnp.sin(ang)


def apply_rope(x, cos, sin):
    xf = x.astype(jnp.float32)
    x1, x2 = jnp.split(xf, 2, axis=-1)
    out = jnp.concatenate([x1 * cos - x2 * sin, x2 * cos + x1 * sin], axis=-1)
    return out.astype(x.dtype)


def split_heads(t, n_heads, head_dim):
    b, s, _ = t.shape
    return t.reshape(b, s, n_heads, head_dim).transpose(0, 2, 1, 3)


def merge_heads(t):
    b, h, s, d = t.shape
    return t.transpose(0, 2, 1, 3).reshape(b, s, h * d)


def dilated_window_attention(q, k, v, window, dilation):
    b, h, s, d = q.shape
    span = window // dilation
    L = s // dilation
    nb = -(-L // DIL_BLOCK)
    Lp = nb * DIL_BLOCK

    def to_blocks(t):
        t = t.reshape(b, h, L, dilation, d).transpose(0, 1, 3, 2, 4)
        t = jnp.pad(t, ((0, 0), (0, 0), (0, 0), (0, Lp - L), (0, 0)))
        return t.reshape(b, h, dilation, nb, DIL_BLOCK, d)

    qb, kb, vb = to_blocks(q), to_blocks(k), to_blocks(v)

    def with_prev(t):
        prev = jnp.pad(t, ((0, 0), (0, 0), (0, 0), (1, 0), (0, 0), (0, 0)))[:, :, :, :-1]
        return jnp.concatenate([prev, t], axis=4)

    kc, vc = with_prev(kb), with_prev(vb)
    scores = jnp.einsum('bhrnqd,bhrnkd->bhrnqk', qb, kc).astype(jnp.float32) * (d ** -0.5)
    qi = jnp.arange(DIL_BLOCK)[:, None]
    ki = jnp.arange(2 * DIL_BLOCK)[None, :] - DIL_BLOCK
    rel = qi - ki
    key_idx = jnp.arange(nb)[:, None, None] * DIL_BLOCK + ki[None]
    mask = (rel >= 0) & (rel <= span) & (key_idx >= 0)
    scores = jnp.where(mask, scores, NEG_INF)
    lse = jax.nn.logsumexp(scores, axis=-1)
    probs = jnp.exp(scores - lse[..., None])
    out = jnp.einsum('bhrnqk,bhrnkd->bhrnqd', probs, vc.astype(jnp.float32))

    def from_blocks(t):
        t = t.reshape(b, h, dilation, Lp, t.shape[-1])[:, :, :, :L]
        return t.transpose(0, 1, 3, 2, 4).reshape(b, h, s, t.shape[-1])

    return from_blocks(out), from_blocks(lse[..., None])[..., 0]


def longnet_attention(q, k, v):
    outs, lses = [], []
    for window, dilation in DILATED_PATTERNS:
        o, l = dilated_window_attention(q, k, v, window, dilation)
        outs.append(o)
        lses.append(l)
    weights = jax.nn.softmax(jnp.stack(lses), axis=0)
    return jnp.einsum('pbhs,pbhsd->bhsd', weights, jnp.stack(outs)).astype(q.dtype)


def differential_attention(q, k, v, lam):
    b, h2, s, d = q.shape
    h = h2 // 2
    nb = s // Q_BLOCK
    q_blocks = q.reshape(b, h2, nb, Q_BLOCK, d).transpose(2, 0, 1, 3, 4)
    key_pos = jnp.arange(s)
    vf = v.astype(jnp.float32)

    def one_block(args):
        q_blk, blk_idx = args
        sc = jnp.einsum('bhqd,bhkd->bhqk', q_blk, k).astype(jnp.float32) * (d ** -0.5)
        q_pos = blk_idx * Q_BLOCK + jnp.arange(Q_BLOCK)
        sc = jnp.where(key_pos[None, :] <= q_pos[:, None], sc, NEG_INF)
        p = jax.nn.softmax(sc, axis=-1).reshape(b, h, 2, Q_BLOCK, s)
        w = p[:, :, 0] - lam * p[:, :, 1]
        return jnp.einsum('bhqk,bhkd->bhqd', w, vf)

    out = lax.map(one_block, (q_blocks, jnp.arange(nb)))
    return out.transpose(1, 2, 0, 3, 4).reshape(b, h, s, v.shape[-1])


def attention_mixer(h, w_in, lambda_params, subln_gain, w_out, lambda_init, cos, sin):
    proj = h @ w_in
    cuts = [A_WIDTH, 2 * A_WIDTH, 3 * A_WIDTH, 3 * A_WIDTH + DIFF_QK_WIDTH,
            3 * A_WIDTH + 2 * DIFF_QK_WIDTH]
    qa, ka, va, qd, kd, vd = jnp.split(proj, cuts, axis=-1)
    qa = apply_rope(split_heads(qa, A_HEADS, HEAD_DIM), cos, sin)
    ka = apply_rope(split_heads(ka, A_HEADS, HEAD_DIM), cos, sin)
    oa = longnet_attention(qa, ka, split_heads(va, A_HEADS, HEAD_DIM))
    qd = apply_rope(split_heads(qd, 2 * DIFF_HEADS, HEAD_DIM), cos, sin)
    kd = apply_rope(split_heads(kd, 2 * DIFF_HEADS, HEAD_DIM), cos, sin)
    lp = lambda_params.astype(jnp.float32)
    lam = jnp.exp(jnp.sum(lp[0] * lp[1])) - jnp.exp(jnp.sum(lp[2] * lp[3])) + lambda_init
    od = differential_attention(qd, kd, split_heads(vd, DIFF_HEADS, DIFF_V_DIM), lam)
    od = rms_norm(od, subln_gain) * (1.0 - lambda_init)
    mixed = jnp.concatenate([merge_heads(oa), merge_heads(od).astype(h.dtype)], axis=-1)
    return mixed @ w_out


def hgrn2_chunkwise(q, k, v, log_f):
    b, h, s, kd = q.shape
    vd = v.shape[-1]
    n = s // HGRN_CHUNK

    def chunks(t):
        return t.reshape(b, h, n, HGRN_CHUNK, t.shape[-1]).transpose(2, 0, 1, 3, 4)

    causal = jnp.tril(jnp.ones((HGRN_CHUNK, HGRN_CHUNK), dtype=bool))[:, :, None]

    def step(state, inp):
        qc, kc, vc, gc = inp
        bcum = jnp.cumsum(gc, axis=2)
        inter = jnp.einsum('bhtk,bhkv->bhtv', qc * jnp.exp(bcum), state)
        diff = bcum[:, :, :, None, :] - bcum[:, :, None, :, :]
        decay = jnp.where(causal, jnp.exp(jnp.where(causal, diff, 0.0)), 0.0)
        attn = jnp.einsum('bhtk,bhsk,bhtsk->bhts', qc, kc, decay)
        intra = jnp.einsum('bhts,bhsv->bhtv', attn, vc)
        b_last = bcum[:, :, -1]
        state = jnp.exp(b_last)[..., None] * state + jnp.einsum(
            'bhsk,bhsv->bhkv', kc * jnp.exp(b_last[:, :, None, :] - bcum), vc)
        return state, inter + intra

    state0 = jnp.zeros((b, h, kd, vd), jnp.float32)
    _, out = lax.scan(step, state0, (chunks(q), chunks(k), chunks(v), chunks(log_f)))
    return out.transpose(1, 2, 0, 3, 4).reshape(b, h, s, vd)


def hgrn2_mixer(h, w_in, lower_bound, g_gain, w_out):
    b, s, _ = h.shape
    proj = h @ w_in
    q, f, i, g = jnp.split(proj, [HGRN_FORGET_WIDTH, 2 * HGRN_FORGET_WIDTH,
                                  2 * HGRN_FORGET_WIDTH + HGRN_VAL_WIDTH], axis=-1)
    forget = lower_bound + (1.0 - lower_bound) * jax.nn.sigmoid(f.astype(jnp.float32))
    log_f = jnp.log(forget)
    k = 1.0 - forget
    qf = jax.nn.silu(q.astype(jnp.float32)) * (HGRN_KEY_DIM ** -0.5)
    o = hgrn2_chunkwise(split_heads(qf, HGRN_HEADS, HGRN_KEY_DIM),
                        split_heads(k, HGRN_HEADS, HGRN_KEY_DIM),
                        split_heads(i.astype(jnp.float32), HGRN_HEADS, HGRN_VAL_DIM),
                        split_heads(log_f, HGRN_HEADS, HGRN_KEY_DIM))
    o = o.transpose(0, 2, 1, 3)
    gate = jax.nn.silu(g.astype(jnp.float32)).reshape(b, s, HGRN_HEADS, HGRN_VAL_DIM)
    o = rms_norm(o, g_gain) * gate
    return o.reshape(b, s, REC_MIX_WIDTH).astype(h.dtype) @ w_out


def swiglu_ffn(h, w_in, w_out):
    gate, up = jnp.split(h @ w_in, 2, axis=-1)
    return (jax.nn.silu(gate) * up) @ w_out


def setup_inputs(seed: int = 0) -> dict:
    key = jax.random.key(seed)
    ks = jax.random.split(key, 12)

    def dense(k, shape, fan_in):
        return jax.random.normal(k, shape, jnp.float32) * fan_in ** -0.5

    x = jax.random.normal(ks[0], (BATCH, SEQ, D_MODEL), jnp.float32)
    norm_gains = 1.0 + 0.05 * jax.random.normal(ks[1], (DEPTH, 4, D_MODEL), jnp.float32)
    att_w_in = dense(ks[2], (N_ATT_LAYERS, D_MODEL, ATT_IN_WIDTH), D_MODEL)
    att_lambda = 0.1 * jax.random.normal(ks[3], (N_ATT_LAYERS, 4, HEAD_DIM), jnp.float32)
    att_subln = 1.0 + 0.05 * jax.random.normal(ks[4], (N_ATT_LAYERS, DIFF_V_DIM), jnp.float32)
    att_w_out = dense(ks[5], (N_ATT_LAYERS, ATT_MIX_WIDTH, D_MODEL), ATT_MIX_WIDTH)
    rec_w_in = dense(ks[6], (N_REC_LAYERS, D_MODEL, REC_IN_WIDTH), D_MODEL)
    rec_lower_bounds = 0.5 * jax.random.normal(ks[7], (N_REC_LAYERS, HGRN_FORGET_WIDTH), jnp.float32)
    rec_gnorm = 1.0 + 0.05 * jax.random.normal(ks[8], (N_REC_LAYERS, HGRN_VAL_DIM), jnp.float32)
    rec_w_out = dense(ks[9], (N_REC_LAYERS, REC_MIX_WIDTH, D_MODEL), REC_MIX_WIDTH)
    ffn_w_in = dense(ks[10], (DEPTH, D_MODEL, 2 * FFN_HIDDEN), D_MODEL)
    ffn_w_out = dense(ks[11], (DEPTH, FFN_HIDDEN, D_MODEL), FFN_HIDDEN)
    return {'x': x, 'norm_gains': norm_gains, 'att_w_in': att_w_in, 'att_lambda': att_lambda,
            'att_subln': att_subln, 'att_w_out': att_w_out, 'rec_w_in': rec_w_in,
            'rec_lower_bounds': rec_lower_bounds, 'rec_gnorm': rec_gnorm, 'rec_w_out': rec_w_out,
            'ffn_w_in': ffn_w_in, 'ffn_w_out': ffn_w_out}


def reference(x, norm_gains, att_w_in, att_lambda, att_subln, att_w_out, rec_w_in,
              rec_lower_bounds, rec_gnorm, rec_w_out, ffn_w_in, ffn_w_out):
    cos, sin = rope_tables(x.shape[1], HEAD_DIM)
    lb_p = jax.nn.softmax(rec_lower_bounds.astype(jnp.float32), axis=0)
    lower_bounds = jnp.cumsum(lb_p, axis=0) - lb_p[0]
    for layer in range(DEPTH):
        gains = norm_gains[layer]
        j = layer // 2
        h = rms_norm(x, gains[0])
        if layer % 2 == 0:
            lambda_init = 0.8 - 0.6 * math.exp(-0.3 * layer)
            m = attention_mixer(h, att_w_in[j], att_lambda[j], att_subln[j], att_w_out[j],
                                lambda_init, cos, sin)
        else:
            m = hgrn2_mixer(h, rec_w_in[j], lower_bounds[j], rec_gnorm[j], rec_w_out[j])
        x = x + rms_norm(m, gains[1])
        h = rms_norm(x, gains[2])
        x = x + rms_norm(swiglu_ffn(h, ffn_w_in[layer], ffn_w_out[layer]), gains[3])
    return x
```

```python
import functools
import math

import jax
import jax.numpy as jnp
from jax import lax
from jax.experimental import pallas as pl
from jax.experimental.pallas import tpu as pltpu

F32 = jnp.float32
BF16 = jnp.bfloat16

HEAD_DIM = 64
A_HEADS = 8
A_WIDTH = A_HEADS * HEAD_DIM
DILATED_PATTERNS = ((128, 1), (512, 4), (2048, 16))
DIL_BLOCK = 128
DIFF_HEADS = 4
DIFF_QK_WIDTH = 2 * DIFF_HEADS * HEAD_DIM
DIFF_V_DIM = 2 * HEAD_DIM
DIFF_V_WIDTH = DIFF_HEADS * DIFF_V_DIM
ROPE_THETA = 10000.0
HGRN_HEADS = 8
HGRN_KEY_DIM = 128
HGRN_VAL_DIM = 128
NORM_EPS = 1e-6
NEG_INF = -1e30

LANES = 128
V7X_VMEM_BYTES = 64 * 1024 * 1024
VMEM_LIMIT = V7X_VMEM_BYTES * 7 // 8

ROW_TILE = 512
COL_CHUNK = 512
DIFF_Q_TILE = 256
HGRN_CHUNK = 64
HGRN_SUB = 16


def _rms(x, gain):
    ms = jnp.mean(x * x, axis=-1, keepdims=True)
    return x * lax.rsqrt(ms + NORM_EPS) * gain


def _sigmoid(x):
    return 1.0 / (1.0 + jnp.exp(-x))


def _dot(a, b):
    return jnp.dot(a, b, preferred_element_type=F32)


def _dot_nt(a, b):
    return lax.dot_general(a, b, (((1,), (1,)), ((), ())), preferred_element_type=F32)


def _dot_tn(a, b):
    return lax.dot_general(a, b, (((0,), (0,)), ((), ())), preferred_element_type=F32)


def _params(semantics):
    return pltpu.CompilerParams(dimension_semantics=semantics, vmem_limit_bytes=VMEM_LIMIT)


def _resident(shape):
    return pl.BlockSpec(shape, lambda *_: (0,) * len(shape), pipeline_mode=pl.Buffered(1))


def _att_in_kernel(x_ref, g_ref, w_ref, cos_ref, sin_ref, o_ref):
    h = _rms(x_ref[...], g_ref[...]).astype(BF16)
    tm = h.shape[0]
    cos = cos_ref[...]
    sin = sin_ref[...]
    lane = lax.broadcasted_iota(jnp.int32, (tm, LANES), 1)
    first_half = (lane % HEAD_DIM) < (HEAD_DIM // 2)
    n_cols = w_ref.shape[1]
    q_scale = HEAD_DIM ** -0.5
    rope_hi = 2 * A_WIDTH
    diff_lo = 3 * A_WIDTH
    diff_rope_hi = diff_lo + 2 * DIFF_QK_WIDTH
    for c0 in range(0, n_cols, COL_CHUNK):
        acc = _dot(h, w_ref[:, c0:c0 + COL_CHUNK])
        for l0 in range(0, COL_CHUNK, LANES):
            col = c0 + l0
            y = acc[:, l0:l0 + LANES]
            if col < rope_hi or diff_lo <= col < diff_rope_hi:
                rot = jnp.where(first_half, pltpu.roll(y, LANES - HEAD_DIM // 2, 1),
                                pltpu.roll(y, HEAD_DIM // 2, 1))
                y = y * cos + rot * sin
                if col < A_WIDTH or diff_lo <= col < diff_lo + DIFF_QK_WIDTH:
                    y = y * q_scale
            o_ref[:, col:col + LANES] = y.astype(BF16)


def _att_in_proj(x2d, gain, w, cos, sin, seq):
    t, d = x2d.shape
    n = w.shape[1]
    tm = min(ROW_TILE, seq)
    tiles_per_seq = seq // tm
    return pl.pallas_call(
        _att_in_kernel,
        grid=(t // tm,),
        in_specs=[pl.BlockSpec((tm, d), lambda i: (i, 0)),
                  _resident((1, d)),
                  _resident((d, n)),
                  pl.BlockSpec((tm, LANES), lambda i: (i % tiles_per_seq, 0)),
                  pl.BlockSpec((tm, LANES), lambda i: (i % tiles_per_seq, 0))],
        out_specs=pl.BlockSpec((tm, n), lambda i: (i, 0)),
        out_shape=jax.ShapeDtypeStruct((t, n), BF16),
        compiler_params=_params(("parallel",)),
        name="att_in_proj",
    )(x2d, gain, w, cos, sin)


def _dilated_kernel(q_ref, k_ref, v_ref, o_ref, qf, kf, vf, acc, mx, den):
    seq = q_ref.shape[1]
    blk = DIL_BLOCK
    qf[...] = q_ref[0].astype(F32)
    kf[...] = k_ref[0].astype(F32)
    vf[...] = v_ref[0].astype(F32)

    lane = lax.broadcasted_iota(jnp.int32, (blk, LANES), 1)
    head_a = lane < HEAD_DIM
    qi = lax.broadcasted_iota(jnp.int32, (2 * blk, 2 * blk), 0) % blk
    ci = lax.broadcasted_iota(jnp.int32, (2 * blk, 2 * blk), 1)
    mask_two = (ci >= qi) & (ci <= qi + blk)
    qi1 = lax.broadcasted_iota(jnp.int32, (2 * blk, blk), 0) % blk
    ci1 = lax.broadcasted_iota(jnp.int32, (2 * blk, blk), 1)
    mask_one = ci1 <= qi1

    for p, (window, dil) in enumerate(DILATED_PATTERNS):
        assert window // dil == blk
        length = seq // dil
        nb = length // blk

        def residue(r, carry, p=p, dil=dil, length=length, nb=nb):
            if dil == 1:
                qr, kr, vr = qf[...], kf[...], vf[...]
            else:
                qr = qf[pl.ds(r, length, stride=dil), :]
                kr = kf[pl.ds(r, length, stride=dil), :]
                vr = vf[pl.ds(r, length, stride=dil), :]
            qr = qr.astype(BF16)
            kr = kr.astype(BF16)
            vr = vr.astype(BF16)
            zero = jnp.zeros((blk, LANES), BF16)
            for n in range(nb):
                qb = qr[n * blk:(n + 1) * blk]
                q2 = jnp.concatenate([jnp.where(head_a, qb, zero), jnp.where(head_a, zero, qb)], axis=0)
                lo = max(n - 1, 0) * blk
                kc = kr[lo:(n + 1) * blk]
                vc = vr[lo:(n + 1) * blk]
                s = _dot_nt(q2, kc)
                s = jnp.where(mask_one if n == 0 else mask_two, s, NEG_INF)
                m = jnp.max(s, axis=-1, keepdims=True)
                e = jnp.exp(s - m)
                l = jnp.sum(e, axis=-1, keepdims=True)
                o = _dot(e.astype(BF16), vc)
                o_nat = jnp.where(head_a, o[:blk], o[blk:])
                m_nat = jnp.where(head_a, m[:blk], m[blk:])
                l_nat = jnp.where(head_a, l[:blk], l[blk:])
                if dil == 1:
                    rows = pl.ds(n * blk, blk)
                else:
                    rows = pl.ds(r + n * blk * dil, blk, stride=dil)
                acc[p, rows, :] = o_nat
                mx[p, rows, :] = m_nat
                den[p, rows, :] = l_nat
            return carry

        if dil == 1:
            residue(0, 0)
        else:
            lax.fori_loop(0, dil, residue, 0)

    m_all = jnp.maximum(jnp.maximum(mx[0], mx[1]), mx[2])
    num = jnp.zeros((seq, LANES), F32)
    tot = jnp.zeros((seq, LANES), F32)
    for p in range(len(DILATED_PATTERNS)):
        w = jnp.exp(mx[p] - m_all)
        num = num + w * acc[p]
        tot = tot + w * den[p]
    o_ref[0] = (num / tot).astype(BF16)


def _dilated_attention(proj):
    b, seq, _ = proj.shape
    n_pairs = A_WIDTH // LANES
    spec = lambda off: pl.BlockSpec((1, seq, LANES), lambda bi, pi: (bi, 0, off + pi))
    n_pat = len(DILATED_PATTERNS)
    return pl.pallas_call(
        _dilated_kernel,
        grid=(b, n_pairs),
        in_specs=[spec(0), spec(n_pairs), spec(2 * n_pairs)],
        out_specs=pl.BlockSpec((1, seq, LANES), lambda bi, pi: (bi, 0, pi)),
        out_shape=jax.ShapeDtypeStruct((b, seq, A_WIDTH), BF16),
        scratch_shapes=[pltpu.VMEM((seq, LANES), F32)] * 3
        + [pltpu.VMEM((n_pat, seq, LANES), F32)] * 3,
        compiler_params=_params(("parallel", "parallel")),
        name="dilated_attention",
    )(proj, proj, proj)


def _diff_kernel(lam_ref, q_ref, k_ref, v_ref, sg_ref, o_ref, *, lambda_init):
    lp = lam_ref[...]
    l1 = jnp.sum(lp[0:1] * lp[1:2], axis=-1, keepdims=True)
    l2 = jnp.sum(lp[2:3] * lp[3:4], axis=-1, keepdims=True)
    lam = jnp.exp(l1) - jnp.exp(l2) + lambda_init
    seq = q_ref.shape[1]
    tq = min(DIFF_Q_TILE, seq)
    lane = lax.broadcasted_iota(jnp.int32, (tq, LANES), 1)
    sub0 = lane < HEAD_DIM
    zero = jnp.zeros((tq, LANES), BF16)
    gain = sg_ref[...] * (1.0 - lambda_init)
    for i in range(seq // tq):
        q = q_ref[0, i * tq:(i + 1) * tq, :]
        q2 = jnp.concatenate([jnp.where(sub0, q, zero), jnp.where(sub0, zero, q)], axis=0)
        nk = (i + 1) * tq
        s = _dot_nt(q2, k_ref[0, 0:nk, :])
        row = lax.broadcasted_iota(jnp.int32, (2 * tq, nk), 0) % tq + i * tq
        col = lax.broadcasted_iota(jnp.int32, (2 * tq, nk), 1)
        s = jnp.where(col <= row, s, NEG_INF)
        m = jnp.max(s, axis=-1, keepdims=True)
        e = jnp.exp(s - m)
        l = jnp.sum(e, axis=-1, keepdims=True)
        o = _dot(e.astype(BF16), v_ref[0, 0:nk, :]) / l
        w = o[:tq] - lam * o[tq:]
        o_ref[0, i * tq:(i + 1) * tq, :] = _rms(w, gain).astype(BF16)


def _diff_attention(proj, lam_params, subln, lambda_init):
    b, seq, _ = proj.shape
    q_off = 3 * A_WIDTH // LANES
    k_off = q_off + DIFF_QK_WIDTH // LANES
    v_off = k_off + DIFF_QK_WIDTH // LANES
    spec = lambda off: pl.BlockSpec((1, seq, LANES), lambda bi, hi: (bi, 0, off + hi))
    return pl.pallas_call(
        functools.partial(_diff_kernel, lambda_init=lambda_init),
        grid=(b, DIFF_HEADS),
        in_specs=[_resident(lam_params.shape), spec(q_off), spec(k_off), spec(v_off),
                  _resident(subln.shape)],
        out_specs=pl.BlockSpec((1, seq, LANES), lambda bi, hi: (bi, 0, hi)),
        out_shape=jax.ShapeDtypeStruct((b, seq, DIFF_V_WIDTH), BF16),
        compiler_params=_params(("parallel", "parallel")),
        name="diff_attention",
    )(lam_params, proj, proj, proj, subln)


def _rec_in_kernel(x_ref, g_ref, w_ref, lb_ref, o_ref, lf_ref, *, layer):
    h = _rms(x_ref[...], g_ref[...]).astype(BF16)
    lb_raw = lb_ref[...]
    lb_e = jnp.exp(lb_raw - jnp.max(lb_raw, axis=0, keepdims=True))
    lb_p = lb_e / jnp.sum(lb_e, axis=0, keepdims=True)
    lower = jnp.sum(lb_p[0:layer + 1], axis=0, keepdims=True) - lb_p[0:1]
    width = lf_ref.shape[1]
    n_cols = w_ref.shape[1]
    q_scale = HGRN_KEY_DIM ** -0.5
    for c0 in range(0, n_cols, COL_CHUNK):
        y = _dot(h, w_ref[:, c0:c0 + COL_CHUNK])
        group = c0 // width
        if group == 0:
            o_ref[:, c0:c0 + COL_CHUNK] = (y * _sigmoid(y) * q_scale).astype(BF16)
        elif group == 1:
            f0 = c0 - width
            lo = lower[:, f0:f0 + COL_CHUNK]
            forget = lo + (1.0 - lo) * _sigmoid(y)
            lf_ref[:, f0:f0 + COL_CHUNK] = jnp.log(forget)
            o_ref[:, c0:c0 + COL_CHUNK] = (1.0 - forget).astype(BF16)
        elif group == 2:
            o_ref[:, c0:c0 + COL_CHUNK] = y.astype(BF16)
        else:
            o_ref[:, c0:c0 + COL_CHUNK] = (y * _sigmoid(y)).astype(BF16)


def _rec_in_proj(x2d, gain, w, lower_bounds, layer, seq):
    t, d = x2d.shape
    n = w.shape[1]
    width = lower_bounds.shape[1]
    tm = min(ROW_TILE, seq)
    return pl.pallas_call(
        functools.partial(_rec_in_kernel, layer=layer),
        grid=(t // tm,),
        in_specs=[pl.BlockSpec((tm, d), lambda i: (i, 0)),
                  _resident((1, d)),
                  _resident((d, n)),
                  _resident(lower_bounds.shape)],
        out_specs=[pl.BlockSpec((tm, n), lambda i: (i, 0)),
                   pl.BlockSpec((tm, width), lambda i: (i, 0))],
        out_shape=[jax.ShapeDtypeStruct((t, n), BF16),
                   jax.ShapeDtypeStruct((t, width), F32)],
        compiler_params=_params(("parallel",)),
        name="rec_in_proj",
    )(x2d, gain, w, lower_bounds)


def _hgrn_kernel(q_ref, k_ref, v_ref, gate_ref, lf_ref, gn_ref, o_ref, st_ref):
    seq = q_ref.shape[1]
    ch = min(HGRN_CHUNK, seq)
    sub = HGRN_SUB
    n_sub = ch // sub
    st_ref[...] = jnp.zeros_like(st_ref)
    gain = gn_ref[...]

    ri = lax.broadcasted_iota(jnp.int32, (ch, ch), 0)
    cj = lax.broadcasted_iota(jnp.int32, (ch, ch), 1)
    tri = jnp.where(cj <= ri, 1.0, 0.0).astype(BF16)
    t_sub = lax.broadcasted_iota(jnp.int32, (sub, ch), 0)
    c_sub = lax.broadcasted_iota(jnp.int32, (sub, ch), 1)

    def chunk(ci, carry):
        r0 = pl.multiple_of(ci * ch, ch)
        rows = pl.ds(r0, ch)
        g = lf_ref[0, rows, :]
        q = q_ref[0, rows, :].astype(F32)
        kb = k_ref[0, rows, :]
        k = kb.astype(F32)
        v = v_ref[0, rows, :]

        g_hi = g.astype(BF16)
        rem = g - g_hi.astype(F32)
        g_mid = rem.astype(BF16)
        g_lo = (rem - g_mid.astype(F32)).astype(BF16)
        parts = _dot(tri, jnp.concatenate([g_hi, g_mid, g_lo], axis=1))
        b = parts[:, :LANES] + parts[:, LANES:2 * LANES] + parts[:, 2 * LANES:]
        b_last = b[ch - 1:ch, :]

        st = st_ref[...]
        inter = _dot_nt((q * jnp.exp(b)).astype(BF16), st.astype(BF16))
        k_dec = (k * jnp.exp(b_last - b)).astype(BF16)
        st_ref[...] = st * jnp.exp(b_last) + _dot_tn(v, k_dec)

        row_blocks = []
        for bi in range(n_sub):
            lo = bi * sub
            q_i = q[lo:lo + sub]
            b_i = b[lo:lo + sub]
            diag = jnp.zeros((sub, ch), F32)
            for s in range(sub):
                dec = jnp.exp(jnp.minimum(b_i - b_i[s:s + 1, :], 0.0))
                z = _dot_nt((q_i * dec).astype(BF16), kb)
                pick = (c_sub == lo + s) & (t_sub >= s)
                diag = diag + jnp.where(pick, z, 0.0)
            if bi > 0:
                ref_b = b[lo - 1:lo, :]
                q_t = (q_i * jnp.exp(b_i - ref_b)).astype(BF16)
                k_t = (k * jnp.exp(jnp.minimum(ref_b - b, 0.0))).astype(BF16)
                diag = diag + jnp.where(c_sub < lo, _dot_nt(q_t, k_t), 0.0)
            row_blocks.append(diag)
        attn = jnp.concatenate(row_blocks, axis=0)
        o = inter + _dot(attn.astype(BF16), v)
        y = _rms(o, gain) * gate_ref[0, rows, :].astype(F32)
        o_ref[0, rows, :] = y.astype(BF16)
        return carry

    lax.fori_loop(0, seq // ch, chunk, 0)


def _hgrn_recurrence(proj, logf, gnorm):
    b, seq, _ = proj.shape
    h = HGRN_HEADS
    spec = lambda off: pl.BlockSpec((1, seq, LANES), lambda bi, hi: (bi, 0, off + hi))
    return pl.pallas_call(
        _hgrn_kernel,
        grid=(b, h),
        in_specs=[spec(0), spec(h), spec(2 * h), spec(3 * h), spec(0), _resident(gnorm.shape)],
        out_specs=pl.BlockSpec((1, seq, LANES), lambda bi, hi: (bi, 0, hi)),
        out_shape=jax.ShapeDtypeStruct((b, seq, h * HGRN_VAL_DIM), BF16),
        scratch_shapes=[pltpu.VMEM((HGRN_VAL_DIM, HGRN_KEY_DIM), F32)],
        compiler_params=_params(("parallel", "parallel")),
        name="hgrn_recurrence",
    )(proj, proj, proj, proj, logf, gnorm)


def _post_kernel(*refs, n_mix, hidden, chunks):
    mix_refs = refs[:n_mix]
    wo_ref, x_ref, g_ref, win_ref, wout_ref, o_ref = refs[n_mix:]
    gains = g_ref[...]
    mixed = jnp.concatenate([r[...] for r in mix_refs], axis=-1) if n_mix > 1 else mix_refs[0][...]
    x1 = x_ref[...] + _rms(_dot(mixed, wo_ref[...]), gains[1:2])
    h = _rms(x1, gains[2:3]).astype(BF16)
    acc = jnp.zeros(x1.shape, F32)
    for lo, hi in chunks:
        gate = _dot(h, win_ref[:, lo:hi])
        up = _dot(h, win_ref[:, hidden + lo:hidden + hi])
        act = (gate * _sigmoid(gate) * up).astype(BF16)
        acc = acc + _dot(act, wout_ref[lo:hi, :])
    o_ref[...] = x1 + _rms(acc, gains[3:4])


def _post_mixer(mix_list, w_out, x2d, gains, ffn_w_in, ffn_w_out, seq):
    t, d = x2d.shape
    hidden = ffn_w_out.shape[0]
    tm = min(ROW_TILE, seq)
    chunks = tuple((lo, min(lo + COL_CHUNK, hidden)) for lo in range(0, hidden, COL_CHUNK))
    mix_specs = [pl.BlockSpec((tm, m.shape[1]), lambda i: (i, 0)) for m in mix_list]
    return pl.pallas_call(
        functools.partial(_post_kernel, n_mix=len(mix_list), hidden=hidden, chunks=chunks),
        grid=(t // tm,),
        in_specs=mix_specs + [_resident(w_out.shape),
                              pl.BlockSpec((tm, d), lambda i: (i, 0)),
                              _resident(gains.shape),
                              _resident(ffn_w_in.shape),
                              _resident(ffn_w_out.shape)],
        out_specs=pl.BlockSpec((tm, d), lambda i: (i, 0)),
        out_shape=jax.ShapeDtypeStruct((t, d), F32),
        compiler_params=_params(("parallel",)),
        name="post_mixer_ffn",
    )(*mix_list, w_out, x2d, gains, ffn_w_in, ffn_w_out)


def _rope_tables(seq):
    half = HEAD_DIM // 2
    inv_freq = ROPE_THETA ** (-jnp.arange(0, HEAD_DIM, 2, dtype=F32) / HEAD_DIM)
    ang = jnp.arange(seq, dtype=F32)[:, None] * inv_freq[None, :]
    reps = LANES // half
    sign = jnp.tile(jnp.concatenate([-jnp.ones((half,), F32), jnp.ones((half,), F32)]), LANES // HEAD_DIM)
    return jnp.tile(jnp.cos(ang), (1, reps)), jnp.tile(jnp.sin(ang), (1, reps)) * sign[None, :]


def kernel(x, norm_gains, att_w_in, att_lambda, att_subln, att_w_out, rec_w_in, rec_lower_bounds,
           rec_gnorm, rec_w_out, ffn_w_in, ffn_w_out):
    b, seq, d = x.shape
    depth = norm_gains.shape[0]
    cos, sin = _rope_tables(seq)
    x2d = x.reshape(b * seq, d)
    for layer in range(depth):
        gains = norm_gains[layer]
        j = layer // 2
        if layer % 2 == 0:
            lambda_init = 0.8 - 0.6 * math.exp(-0.3 * layer)
            proj = _att_in_proj(x2d, gains[0:1], att_w_in[j].astype(BF16), cos, sin, seq)
            proj = proj.reshape(b, seq, -1)
            oa = _dilated_attention(proj)
            od = _diff_attention(proj, att_lambda[j], att_subln[j][None, :], lambda_init)
            mix = [oa.reshape(b * seq, -1), od.reshape(b * seq, -1)]
            w_out = att_w_out[j]
        else:
            proj, logf = _rec_in_proj(x2d, gains[0:1], rec_w_in[j].astype(BF16), rec_lower_bounds, j, seq)
            o = _hgrn_recurrence(proj.reshape(b, seq, -1), logf.reshape(b, seq, -1), rec_gnorm[j][None, :])
            mix = [o.reshape(b * seq, -1)]
            w_out = rec_w_out[j]
        x2d = _post_mixer(mix, w_out.astype(BF16), x2d, gains, ffn_w_in[layer].astype(BF16),
                          ffn_w_out[layer].astype(BF16), seq)
    return x2d.reshape(b, seq, d)
```

```python
import functools
import math

import jax
import jax.numpy as jnp
from jax import lax
from jax.experimental import pallas as pl
from jax.experimental.pallas import tpu as pltpu

F32 = jnp.float32
BF16 = jnp.bfloat16

HEAD_DIM = 64
A_HEADS = 8
A_WIDTH = A_HEADS * HEAD_DIM
DILATED_PATTERNS = ((128, 1), (512, 4), (2048, 16))
DIL_BLOCK = 128
DIFF_HEADS = 4
DIFF_QK_WIDTH = 2 * DIFF_HEADS * HEAD_DIM
DIFF_V_DIM = 2 * HEAD_DIM
DIFF_V_WIDTH = DIFF_HEADS * DIFF_V_DIM
ROPE_THETA = 10000.0
HGRN_HEADS = 8
HGRN_KEY_DIM = 128
HGRN_VAL_DIM = 128
NORM_EPS = 1e-6
NEG_INF = -1e30

LANES = 128
V7X_VMEM_BYTES = 64 * 1024 * 1024
VMEM_LIMIT = V7X_VMEM_BYTES * 7 // 8

ROW_TILE = 512
COL_CHUNK = 512
DIFF_Q_TILE = 256
HGRN_CHUNK = 64
HGRN_GROUP = 8
HGRN_ROW_GROUP = 8


def _rms(x, gain):
    ms = jnp.mean(x * x, axis=-1, keepdims=True)
    return x * lax.rsqrt(ms + NORM_EPS) * gain


def _sigmoid(x):
    return 1.0 / (1.0 + jnp.exp(-x))


def _dot(a, b):
    return jnp.dot(a, b, preferred_element_type=F32)


def _dot_nt(a, b):
    return lax.dot_general(a, b, (((1,), (1,)), ((), ())), preferred_element_type=F32)


def _dot_tn(a, b):
    return lax.dot_general(a, b, (((0,), (0,)), ((), ())), preferred_element_type=F32)


def _params(semantics):
    return pltpu.CompilerParams(dimension_semantics=semantics, vmem_limit_bytes=VMEM_LIMIT)


def _resident(shape):
    return pl.BlockSpec(shape, lambda *_: (0,) * len(shape), pipeline_mode=pl.Buffered(1))


def _att_in_kernel(x_ref, g_ref, w_ref, cos_ref, sin_ref, o_ref):
    h = _rms(x_ref[...], g_ref[...]).astype(BF16)
    tm = h.shape[0]
    cos = cos_ref[...]
    sin = sin_ref[...]
    lane = lax.broadcasted_iota(jnp.int32, (tm, LANES), 1)
    first_half = (lane % HEAD_DIM) < (HEAD_DIM // 2)
    n_cols = w_ref.shape[1]
    q_scale = HEAD_DIM ** -0.5
    rope_hi = 2 * A_WIDTH
    diff_lo = 3 * A_WIDTH
    diff_rope_hi = diff_lo + 2 * DIFF_QK_WIDTH
    for c0 in range(0, n_cols, COL_CHUNK):
        acc = _dot(h, w_ref[:, c0:c0 + COL_CHUNK])
        for l0 in range(0, COL_CHUNK, LANES):
            col = c0 + l0
            y = acc[:, l0:l0 + LANES]
            if col < rope_hi or diff_lo <= col < diff_rope_hi:
                rot = jnp.where(first_half, pltpu.roll(y, LANES - HEAD_DIM // 2, 1),
                                pltpu.roll(y, HEAD_DIM // 2, 1))
                y = y * cos + rot * sin
                if col < A_WIDTH or diff_lo <= col < diff_lo + DIFF_QK_WIDTH:
                    y = y * q_scale
            o_ref[:, col:col + LANES] = y.astype(BF16)


def _att_in_proj(x2d, gain, w, cos, sin, seq):
    t, d = x2d.shape
    n = w.shape[1]
    tm = min(ROW_TILE, seq)
    tiles_per_seq = seq // tm
    return pl.pallas_call(
        _att_in_kernel,
        grid=(t // tm,),
        in_specs=[pl.BlockSpec((tm, d), lambda i: (i, 0)),
                  _resident((1, d)),
                  _resident((d, n)),
                  pl.BlockSpec((tm, LANES), lambda i: (i % tiles_per_seq, 0)),
                  pl.BlockSpec((tm, LANES), lambda i: (i % tiles_per_seq, 0))],
        out_specs=pl.BlockSpec((tm, n), lambda i: (i, 0)),
        out_shape=jax.ShapeDtypeStruct((t, n), BF16),
        compiler_params=_params(("parallel",)),
        name="att_in_proj",
    )(x2d, gain, w, cos, sin)


def _dilated_kernel(q_ref, k_ref, v_ref, o_ref, qf, kf, vf, acc, mx, den):
    seq = q_ref.shape[1]
    blk = DIL_BLOCK
    qf[...] = q_ref[0].astype(F32)
    kf[...] = k_ref[0].astype(F32)
    vf[...] = v_ref[0].astype(F32)

    lane = lax.broadcasted_iota(jnp.int32, (blk, LANES), 1)
    head_a = lane < HEAD_DIM
    qi = lax.broadcasted_iota(jnp.int32, (2 * blk, 2 * blk), 0) % blk
    ci = lax.broadcasted_iota(jnp.int32, (2 * blk, 2 * blk), 1)
    mask_two = (ci >= qi) & (ci <= qi + blk)
    qi1 = lax.broadcasted_iota(jnp.int32, (2 * blk, blk), 0) % blk
    ci1 = lax.broadcasted_iota(jnp.int32, (2 * blk, blk), 1)
    mask_one = ci1 <= qi1

    for p, (window, dil) in enumerate(DILATED_PATTERNS):
        assert window // dil == blk
        length = seq // dil
        nb = length // blk

        def residue(r, carry, p=p, dil=dil, length=length, nb=nb):
            if dil == 1:
                qr, kr, vr = qf[...], kf[...], vf[...]
            else:
                qr = qf[pl.ds(r, length, stride=dil), :]
                kr = kf[pl.ds(r, length, stride=dil), :]
                vr = vf[pl.ds(r, length, stride=dil), :]
            qr = qr.astype(BF16)
            kr = kr.astype(BF16)
            vr = vr.astype(BF16)
            zero = jnp.zeros((blk, LANES), BF16)
            for n in range(nb):
                qb = qr[n * blk:(n + 1) * blk]
                q2 = jnp.concatenate([jnp.where(head_a, qb, zero), jnp.where(head_a, zero, qb)], axis=0)
                lo = max(n - 1, 0) * blk
                kc = kr[lo:(n + 1) * blk]
                vc = vr[lo:(n + 1) * blk]
                s = _dot_nt(q2, kc)
                s = jnp.where(mask_one if n == 0 else mask_two, s, NEG_INF)
                m = jnp.max(s, axis=-1, keepdims=True)
                e = jnp.exp(s - m)
                l = jnp.sum(e, axis=-1, keepdims=True)
                o = _dot(e.astype(BF16), vc)
                o_nat = jnp.where(head_a, o[:blk], o[blk:])
                m_nat = jnp.where(head_a, m[:blk], m[blk:])
                l_nat = jnp.where(head_a, l[:blk], l[blk:])
                if dil == 1:
                    rows = pl.ds(n * blk, blk)
                else:
                    rows = pl.ds(r + n * blk * dil, blk, stride=dil)
                acc[p, rows, :] = o_nat
                mx[p, rows, :] = m_nat
                den[p, rows, :] = l_nat
            return carry

        if dil == 1:
            residue(0, 0)
        else:
            lax.fori_loop(0, dil, residue, 0)

    m_all = jnp.maximum(jnp.maximum(mx[0], mx[1]), mx[2])
    num = jnp.zeros((seq, LANES), F32)
    tot = jnp.zeros((seq, LANES), F32)
    for p in range(len(DILATED_PATTERNS)):
        w = jnp.exp(mx[p] - m_all)
        num = num + w * acc[p]
        tot = tot + w * den[p]
    o_ref[0] = (num / tot).astype(BF16)


def _dilated_attention(proj):
    b, seq, _ = proj.shape
    n_pairs = A_WIDTH // LANES
    spec = lambda off: pl.BlockSpec((1, seq, LANES), lambda bi, pi: (bi, 0, off + pi))
    n_pat = len(DILATED_PATTERNS)
    return pl.pallas_call(
        _dilated_kernel,
        grid=(b, n_pairs),
        in_specs=[spec(0), spec(n_pairs), spec(2 * n_pairs)],
        out_specs=pl.BlockSpec((1, seq, LANES), lambda bi, pi: (bi, 0, pi)),
        out_shape=jax.ShapeDtypeStruct((b, seq, A_WIDTH), BF16),
        scratch_shapes=[pltpu.VMEM((seq, LANES), F32)] * 3
        + [pltpu.VMEM((n_pat, seq, LANES), F32)] * 3,
        compiler_params=_params(("parallel", "parallel")),
        name="dilated_attention",
    )(proj, proj, proj)


def _diff_kernel(lam_ref, q_ref, k_ref, v_ref, sg_ref, o_ref, *, lambda_init):
    lp = lam_ref[...]
    l1 = jnp.sum(lp[0:1] * lp[1:2], axis=-1, keepdims=True)
    l2 = jnp.sum(lp[2:3] * lp[3:4], axis=-1, keepdims=True)
    lam = jnp.exp(l1) - jnp.exp(l2) + lambda_init
    seq = q_ref.shape[1]
    tq = min(DIFF_Q_TILE, seq)
    lane = lax.broadcasted_iota(jnp.int32, (tq, LANES), 1)
    sub0 = lane < HEAD_DIM
    zero = jnp.zeros((tq, LANES), BF16)
    gain = sg_ref[...] * (1.0 - lambda_init)
    for i in range(seq // tq):
        q = q_ref[0, i * tq:(i + 1) * tq, :]
        q2 = jnp.concatenate([jnp.where(sub0, q, zero), jnp.where(sub0, zero, q)], axis=0)
        nk = (i + 1) * tq
        s = _dot_nt(q2, k_ref[0, 0:nk, :])
        row = lax.broadcasted_iota(jnp.int32, (2 * tq, nk), 0) % tq + i * tq
        col = lax.broadcasted_iota(jnp.int32, (2 * tq, nk), 1)
        s = jnp.where(col <= row, s, NEG_INF)
        m = jnp.max(s, axis=-1, keepdims=True)
        e = jnp.exp(s - m)
        l = jnp.sum(e, axis=-1, keepdims=True)
        o = _dot(e.astype(BF16), v_ref[0, 0:nk, :]) / l
        w = o[:tq] - lam * o[tq:]
        o_ref[0, i * tq:(i + 1) * tq, :] = _rms(w, gain).astype(BF16)


def _diff_attention(proj, lam_params, subln, lambda_init):
    b, seq, _ = proj.shape
    q_off = 3 * A_WIDTH // LANES
    k_off = q_off + DIFF_QK_WIDTH // LANES
    v_off = k_off + DIFF_QK_WIDTH // LANES
    spec = lambda off: pl.BlockSpec((1, seq, LANES), lambda bi, hi: (bi, 0, off + hi))
    return pl.pallas_call(
        functools.partial(_diff_kernel, lambda_init=lambda_init),
        grid=(b, DIFF_HEADS),
        in_specs=[_resident(lam_params.shape), spec(q_off), spec(k_off), spec(v_off),
                  _resident(subln.shape)],
        out_specs=pl.BlockSpec((1, seq, LANES), lambda bi, hi: (bi, 0, hi)),
        out_shape=jax.ShapeDtypeStruct((b, seq, DIFF_V_WIDTH), BF16),
        compiler_params=_params(("parallel", "parallel")),
        name="diff_attention",
    )(lam_params, proj, proj, proj, subln)


def _rec_in_kernel(x_ref, g_ref, w_ref, lb_ref, o_ref, lf_ref, *, layer):
    h = _rms(x_ref[...], g_ref[...]).astype(BF16)
    lb_raw = lb_ref[...]
    lb_e = jnp.exp(lb_raw - jnp.max(lb_raw, axis=0, keepdims=True))
    lb_p = lb_e / jnp.sum(lb_e, axis=0, keepdims=True)
    lower = jnp.sum(lb_p[0:layer + 1], axis=0, keepdims=True) - lb_p[0:1]
    width = lf_ref.shape[1]
    n_cols = w_ref.shape[1]
    q_scale = HGRN_KEY_DIM ** -0.5
    for c0 in range(0, n_cols, COL_CHUNK):
        y = _dot(h, w_ref[:, c0:c0 + COL_CHUNK])
        group = c0 // width
        if group == 0:
            o_ref[:, c0:c0 + COL_CHUNK] = (y * _sigmoid(y) * q_scale).astype(BF16)
        elif group == 1:
            f0 = c0 - width
            lo = lower[:, f0:f0 + COL_CHUNK]
            forget = lo + (1.0 - lo) * _sigmoid(y)
            lf_ref[:, f0:f0 + COL_CHUNK] = jnp.log(forget)
            o_ref[:, c0:c0 + COL_CHUNK] = (1.0 - forget).astype(BF16)
        elif group == 2:
            o_ref[:, c0:c0 + COL_CHUNK] = y.astype(BF16)
        else:
            o_ref[:, c0:c0 + COL_CHUNK] = (y * _sigmoid(y)).astype(BF16)


def _rec_in_proj(x2d, gain, w, lower_bounds, layer, seq):
    t, d = x2d.shape
    n = w.shape[1]
    width = lower_bounds.shape[1]
    tm = min(ROW_TILE, seq)
    return pl.pallas_call(
        functools.partial(_rec_in_kernel, layer=layer),
        grid=(t // tm,),
        in_specs=[pl.BlockSpec((tm, d), lambda i: (i, 0)),
                  _resident((1, d)),
                  _resident((d, n)),
                  _resident(lower_bounds.shape)],
        out_specs=[pl.BlockSpec((tm, n), lambda i: (i, 0)),
                   pl.BlockSpec((tm, width), lambda i: (i, 0))],
        out_shape=[jax.ShapeDtypeStruct((t, n), BF16),
                   jax.ShapeDtypeStruct((t, width), F32)],
        compiler_params=_params(("parallel",)),
        name="rec_in_proj",
    )(x2d, gain, w, lower_bounds)


def _hgrn_kernel(q_ref, k_ref, v_ref, gate_ref, lf_ref, gn_ref, o_ref, st_ref):
    seq = q_ref.shape[1]
    ch = min(HGRN_CHUNK, seq)
    st_ref[...] = jnp.zeros_like(st_ref)
    gain = gn_ref[...]

    ti = lax.broadcasted_iota(jnp.int32, (ch, ch), 0)
    sj = lax.broadcasted_iota(jnp.int32, (ch, ch), 1)
    halves = [ch >> (i + 1) for i in range(ch.bit_length() - 1)]
    in_vreg = [h for h in halves if h < HGRN_ROW_GROUP]

    def ref_row(h):
        return (ti & -(2 * h)) + (h - 1)

    def prefix(idx):
        return jnp.where(sj <= idx, 1.0, 0.0)

    op = jnp.concatenate([prefix(ti)] + [prefix(ti) - prefix(ref_row(h)) for h in in_vreg],
                         axis=0).astype(BF16)
    level_mask = [jnp.where((((ti ^ sj) & -(2 * h)) == 0) & ((ti & h) != 0) & ((sj & h) == 0), 1.0, 0.0)
                  for h in halves]
    eye = jnp.where(ti == sj, 1.0, 0.0)

    group = min(HGRN_GROUP, seq // ch)

    def chunk_rows(gi, c):
        return pl.ds(pl.multiple_of(gi * (group * ch), ch) + c * ch, ch)

    def step(gi, carry):
        cums = []
        for c in range(group):
            g = lf_ref[0, chunk_rows(gi, c), :]
            g_hi = g.astype(BF16)
            g_lo = (g - g_hi.astype(F32)).astype(BF16)
            parts = _dot(op, jnp.concatenate([g_hi, g_lo], axis=1))
            cums.append(parts[:, :LANES] + parts[:, LANES:])

        attns, decays = [], []
        for c in range(group):
            rows = chunk_rows(gi, c)
            qb = q_ref[0, rows, :]
            kb = k_ref[0, rows, :]
            q = qb.astype(F32)
            k = kb.astype(F32)
            cum = cums[c]
            b = cum[:ch]
            b_last = b[ch - 1:ch, :]
            attn = _dot_nt(qb, kb) * eye
            for li, h in enumerate(halves):
                if h in in_vreg:
                    blk = 1 + in_vreg.index(h)
                    dist = cum[blk * ch:(blk + 1) * ch]
                else:
                    refs = [jnp.broadcast_to(b[a + h - 1:a + h, :], (2 * h, LANES))
                            for a in range(0, ch, 2 * h)]
                    dist = b - (jnp.concatenate(refs, axis=0) if len(refs) > 1 else refs[0])
                e = jnp.exp(-jnp.abs(dist))
                attn = attn + _dot_nt((q * e).astype(BF16), (k * e).astype(BF16)) * level_mask[li]
            attns.append(attn.astype(BF16))
            decays.append(((q * jnp.exp(b)).astype(BF16), (k * jnp.exp(b_last - b)).astype(BF16),
                           jnp.exp(b_last)))

        st = st_ref[...]
        for c in range(group):
            rows = chunk_rows(gi, c)
            v = v_ref[0, rows, :]
            q_dec, k_dec, total = decays[c]
            o = _dot_nt(q_dec, st.astype(BF16)) + _dot(attns[c], v)
            st = st * total + _dot_tn(v, k_dec)
            y = _rms(o, gain) * gate_ref[0, rows, :].astype(F32)
            o_ref[0, rows, :] = y.astype(BF16)
        st_ref[...] = st
        return carry

    lax.fori_loop(0, seq // (group * ch), step, 0)


def _hgrn_recurrence(proj, logf, gnorm):
    b, seq, _ = proj.shape
    h = HGRN_HEADS
    spec = lambda off: pl.BlockSpec((1, seq, LANES), lambda bi, hi: (bi, 0, off + hi))
    return pl.pallas_call(
        _hgrn_kernel,
        grid=(b, h),
        in_specs=[spec(0), spec(h), spec(2 * h), spec(3 * h), spec(0), _resident(gnorm.shape)],
        out_specs=pl.BlockSpec((1, seq, LANES), lambda bi, hi: (bi, 0, hi)),
        out_shape=jax.ShapeDtypeStruct((b, seq, h * HGRN_VAL_DIM), BF16),
        scratch_shapes=[pltpu.VMEM((HGRN_VAL_DIM, HGRN_KEY_DIM), F32)],
        compiler_params=_params(("parallel", "parallel")),
        name="hgrn_recurrence",
    )(proj, proj, proj, proj, logf, gnorm)


def _post_kernel(*refs, n_mix, hidden, chunks):
    mix_refs = refs[:n_mix]
    wo_ref, x_ref, g_ref, win_ref, wout_ref, o_ref = refs[n_mix:]
    gains = g_ref[...]
    mixed = jnp.concatenate([r[...] for r in mix_refs], axis=-1) if n_mix > 1 else mix_refs[0][...]
    x1 = x_ref[...] + _rms(_dot(mixed, wo_ref[...]), gains[1:2])
    h = _rms(x1, gains[2:3]).astype(BF16)
    acc = jnp.zeros(x1.shape, F32)
    for lo, hi in chunks:
        gate = _dot(h, win_ref[:, lo:hi])
        up = _dot(h, win_ref[:, hidden + lo:hidden + hi])
        act = (gate * _sigmoid(gate) * up).astype(BF16)
        acc = acc + _dot(act, wout_ref[lo:hi, :])
    o_ref[...] = x1 + _rms(acc, gains[3:4])


def _post_mixer(mix_list, w_out, x2d, gains, ffn_w_in, ffn_w_out, seq):
    t, d = x2d.shape
    hidden = ffn_w_out.shape[0]
    tm = min(ROW_TILE, seq)
    chunks = tuple((lo, min(lo + COL_CHUNK, hidden)) for lo in range(0, hidden, COL_CHUNK))
    mix_specs = [pl.BlockSpec((tm, m.shape[1]), lambda i: (i, 0)) for m in mix_list]
    return pl.pallas_call(
        functools.partial(_post_kernel, n_mix=len(mix_list), hidden=hidden, chunks=chunks),
        grid=(t // tm,),
        in_specs=mix_specs + [_resident(w_out.shape),
                              pl.BlockSpec((tm, d), lambda i: (i, 0)),
                              _resident(gains.shape),
                              _resident(ffn_w_in.shape),
                              _resident(ffn_w_out.shape)],
        out_specs=pl.BlockSpec((tm, d), lambda i: (i, 0)),
        out_shape=jax.ShapeDtypeStruct((t, d), F32),
        compiler_params=_params(("parallel",)),
        name="post_mixer_ffn",
    )(*mix_list, w_out, x2d, gains, ffn_w_in, ffn_w_out)


def _rope_tables(seq):
    half = HEAD_DIM // 2
    inv_freq = ROPE_THETA ** (-jnp.arange(0, HEAD_DIM, 2, dtype=F32) / HEAD_DIM)
    ang = jnp.arange(seq, dtype=F32)[:, None] * inv_freq[None, :]
    reps = LANES // half
    sign = jnp.tile(jnp.concatenate([-jnp.ones((half,), F32), jnp.ones((half,), F32)]), LANES // HEAD_DIM)
    return jnp.tile(jnp.cos(ang), (1, reps)), jnp.tile(jnp.sin(ang), (1, reps)) * sign[None, :]


def kernel(x, norm_gains, att_w_in, att_lambda, att_subln, att_w_out, rec_w_in, rec_lower_bounds,
           rec_gnorm, rec_w_out, ffn_w_in, ffn_w_out):
    b, seq, d = x.shape
    depth = norm_gains.shape[0]
    cos, sin = _rope_tables(seq)
    x2d = x.reshape(b * seq, d)
    for layer in range(depth):
        gains = norm_gains[layer]
        j = layer // 2
        if layer % 2 == 0:
            lambda_init = 0.8 - 0.6 * math.exp(-0.3 * layer)
            proj = _att_in_proj(x2d, gains[0:1], att_w_in[j].astype(BF16), cos, sin, seq)
            proj = proj.reshape(b, seq, -1)
            oa = _dilated_attention(proj)
            od = _diff_attention(proj, att_lambda[j], att_subln[j][None, :], lambda_init)
            mix = [oa.reshape(b * seq, -1), od.reshape(b * seq, -1)]
            w_out = att_w_out[j]
        else:
            proj, logf = _rec_in_proj(x2d, gains[0:1], rec_w_in[j].astype(BF16), rec_lower_bounds, j, seq)
            o = _hgrn_recurrence(proj.reshape(b, seq, -1), logf.reshape(b, seq, -1), rec_gnorm[j][None, :])
            mix = [o.reshape(b * seq, -1)]
            w_out = rec_w_out[j]
        x2d = _post_mixer(mix, w_out.astype(BF16), x2d, gains, ffn_w_in[layer].astype(BF16),
                          ffn_w_out[layer].astype(BF16), seq)
    return x2d.reshape(b, seq, d)
```

```python
import functools
import math

import jax
import jax.numpy as jnp
from jax import lax
from jax.experimental import pallas as pl
from jax.experimental.pallas import tpu as pltpu

F32 = jnp.float32
BF16 = jnp.bfloat16

HEAD_DIM = 64
A_HEADS = 8
A_WIDTH = A_HEADS * HEAD_DIM
DILATED_PATTERNS = ((128, 1), (512, 4), (2048, 16))
DIL_BLOCK = 128
DIFF_HEADS = 4
DIFF_QK_WIDTH = 2 * DIFF_HEADS * HEAD_DIM
DIFF_V_DIM = 2 * HEAD_DIM
DIFF_V_WIDTH = DIFF_HEADS * DIFF_V_DIM
ROPE_THETA = 10000.0
HGRN_HEADS = 8
HGRN_KEY_DIM = 128
HGRN_VAL_DIM = 128
NORM_EPS = 1e-6
NEG_INF = -1e30

LANES = 128
V7X_VMEM_BYTES = 64 * 1024 * 1024
VMEM_LIMIT = V7X_VMEM_BYTES * 7 // 8

ROW_TILE = 512
COL_CHUNK = 512
DIL_GROUP = 4
DIFF_Q_TILE = 256
HGRN_CHUNK = 64
HGRN_GROUP = 8
HGRN_ROW_GROUP = 8


def _rms(x, gain):
    ms = jnp.mean(x * x, axis=-1, keepdims=True)
    return x * lax.rsqrt(ms + NORM_EPS) * gain


def _sigmoid(x):
    return 1.0 / (1.0 + jnp.exp(-x))


def _dot(a, b):
    return jnp.dot(a, b, preferred_element_type=F32)


def _dot_nt(a, b):
    return lax.dot_general(a, b, (((1,), (1,)), ((), ())), preferred_element_type=F32)


def _dot_tn(a, b):
    return lax.dot_general(a, b, (((0,), (0,)), ((), ())), preferred_element_type=F32)


def _params(semantics):
    return pltpu.CompilerParams(dimension_semantics=semantics, vmem_limit_bytes=VMEM_LIMIT)


def _resident(shape):
    return pl.BlockSpec(shape, lambda *_: (0,) * len(shape), pipeline_mode=pl.Buffered(1))


def _att_in_kernel(x_ref, g_ref, w_ref, cos_ref, sin_ref, o_ref):
    h = _rms(x_ref[...], g_ref[...]).astype(BF16)
    tm = h.shape[0]
    cos = cos_ref[...]
    sin = sin_ref[...]
    lane = lax.broadcasted_iota(jnp.int32, (tm, LANES), 1)
    first_half = (lane % HEAD_DIM) < (HEAD_DIM // 2)
    n_cols = w_ref.shape[1]
    q_scale = HEAD_DIM ** -0.5
    rope_hi = 2 * A_WIDTH
    diff_lo = 3 * A_WIDTH
    diff_rope_hi = diff_lo + 2 * DIFF_QK_WIDTH
    for c0 in range(0, n_cols, COL_CHUNK):
        acc = _dot(h, w_ref[:, c0:c0 + COL_CHUNK])
        for l0 in range(0, COL_CHUNK, LANES):
            col = c0 + l0
            y = acc[:, l0:l0 + LANES]
            if col < rope_hi or diff_lo <= col < diff_rope_hi:
                rot = jnp.where(first_half, pltpu.roll(y, LANES - HEAD_DIM // 2, 1),
                                pltpu.roll(y, HEAD_DIM // 2, 1))
                y = y * cos + rot * sin
                if col < A_WIDTH or diff_lo <= col < diff_lo + DIFF_QK_WIDTH:
                    y = y * q_scale
            o_ref[:, col:col + LANES] = y.astype(BF16)


def _att_in_proj(x2d, gain, w, cos, sin, seq):
    t, d = x2d.shape
    n = w.shape[1]
    tm = min(ROW_TILE, seq)
    tiles_per_seq = seq // tm
    return pl.pallas_call(
        _att_in_kernel,
        grid=(t // tm,),
        in_specs=[pl.BlockSpec((tm, d), lambda i: (i, 0)),
                  _resident((1, d)),
                  _resident((d, n)),
                  pl.BlockSpec((tm, LANES), lambda i: (i % tiles_per_seq, 0)),
                  pl.BlockSpec((tm, LANES), lambda i: (i % tiles_per_seq, 0))],
        out_specs=pl.BlockSpec((tm, n), lambda i: (i, 0)),
        out_shape=jax.ShapeDtypeStruct((t, n), BF16),
        compiler_params=_params(("parallel",)),
        name="att_in_proj",
    )(x2d, gain, w, cos, sin)


def _dilated_kernel(q_ref, k_ref, v_ref, o_ref, qf, kf, vf, acc, mx, den):
    seq = q_ref.shape[1]
    blk = DIL_BLOCK
    qf[...] = q_ref[0].astype(F32)
    kf[...] = k_ref[0].astype(F32)
    vf[...] = v_ref[0].astype(F32)

    lane = lax.broadcasted_iota(jnp.int32, (blk, LANES), 1)
    head_a = lane < HEAD_DIM
    qi = lax.broadcasted_iota(jnp.int32, (2 * blk, 2 * blk), 0) % blk
    ci = lax.broadcasted_iota(jnp.int32, (2 * blk, 2 * blk), 1)
    mask_two = (ci >= qi) & (ci <= qi + blk)
    qi1 = lax.broadcasted_iota(jnp.int32, (2 * blk, blk), 0) % blk
    ci1 = lax.broadcasted_iota(jnp.int32, (2 * blk, blk), 1)
    mask_one = ci1 <= qi1

    zero = jnp.zeros((blk, LANES), BF16)

    def attend(p, blocks):
        scores = []
        for qb, kc, _, _, _ in blocks:
            q2 = jnp.concatenate([jnp.where(head_a, qb, zero), jnp.where(head_a, zero, qb)], axis=0)
            scores.append(_dot_nt(q2, kc))
        probs = []
        for s, (_, _, _, first, _) in zip(scores, blocks):
            s = jnp.where(mask_one if first else mask_two, s, NEG_INF)
            m = jnp.max(s, axis=-1, keepdims=True)
            probs.append((jnp.exp(s - m).astype(BF16), m))
        for (e, m), (_, _, vc, _, rows) in zip(probs, blocks):
            o = _dot(e, jnp.concatenate([vc, jnp.ones_like(vc)], axis=1))
            acc[p, rows, :] = jnp.where(head_a, o[:blk, :LANES], o[blk:, :LANES])
            mx[p, rows, :] = jnp.where(head_a, m[:blk], m[blk:])
            den[p, rows, :] = jnp.where(head_a, o[:blk, LANES:], o[blk:, LANES:])

    for p, (window, dil) in enumerate(DILATED_PATTERNS):
        assert window // dil == blk
        length = seq // dil
        nb = length // blk

        if dil == 1:
            for n0 in range(0, nb, DIL_GROUP):
                blocks = []
                for n in range(n0, min(n0 + DIL_GROUP, nb)):
                    keys = slice(max(n - 1, 0) * blk, (n + 1) * blk)
                    blocks.append((q_ref[0, n * blk:(n + 1) * blk, :], k_ref[0, keys, :], v_ref[0, keys, :],
                                   n == 0, pl.ds(n * blk, blk)))
                attend(p, blocks)
            continue

        per_step = max(DIL_GROUP // nb, 1)

        def residues(step, carry, p=p, dil=dil, length=length, nb=nb, per_step=per_step):
            blocks = []
            for j in range(per_step):
                r = step * per_step + j
                qr = qf[pl.ds(r, length, stride=dil), :].astype(BF16)
                kr = kf[pl.ds(r, length, stride=dil), :].astype(BF16)
                vr = vf[pl.ds(r, length, stride=dil), :].astype(BF16)
                for n in range(nb):
                    keys = slice(max(n - 1, 0) * blk, (n + 1) * blk)
                    blocks.append((qr[n * blk:(n + 1) * blk], kr[keys], vr[keys], n == 0,
                                   pl.ds(r + n * blk * dil, blk, stride=dil)))
            attend(p, blocks)
            return carry

        lax.fori_loop(0, dil // per_step, residues, 0)

    m_all = jnp.maximum(jnp.maximum(mx[0], mx[1]), mx[2])
    num = jnp.zeros((seq, LANES), F32)
    tot = jnp.zeros((seq, LANES), F32)
    for p in range(len(DILATED_PATTERNS)):
        w = jnp.exp(mx[p] - m_all)
        num = num + w * acc[p]
        tot = tot + w * den[p]
    o_ref[0] = (num / tot).astype(BF16)


def _dilated_attention(proj):
    b, seq, _ = proj.shape
    n_pairs = A_WIDTH // LANES
    spec = lambda off: pl.BlockSpec((1, seq, LANES), lambda bi, pi: (bi, 0, off + pi))
    n_pat = len(DILATED_PATTERNS)
    return pl.pallas_call(
        _dilated_kernel,
        grid=(b, n_pairs),
        in_specs=[spec(0), spec(n_pairs), spec(2 * n_pairs)],
        out_specs=pl.BlockSpec((1, seq, LANES), lambda bi, pi: (bi, 0, pi)),
        out_shape=jax.ShapeDtypeStruct((b, seq, A_WIDTH), BF16),
        scratch_shapes=[pltpu.VMEM((seq, LANES), F32)] * 3
        + [pltpu.VMEM((n_pat, seq, LANES), F32)] * 3,
        compiler_params=_params(("parallel", "parallel")),
        name="dilated_attention",
    )(proj, proj, proj)


def _diff_kernel(lam_ref, q_ref, k_ref, v_ref, sg_ref, o_ref, *, lambda_init):
    lp = lam_ref[...]
    l1 = jnp.sum(lp[0:1] * lp[1:2], axis=-1, keepdims=True)
    l2 = jnp.sum(lp[2:3] * lp[3:4], axis=-1, keepdims=True)
    lam = jnp.exp(l1) - jnp.exp(l2) + lambda_init
    seq = q_ref.shape[1]
    tq = min(DIFF_Q_TILE, seq)
    lane = lax.broadcasted_iota(jnp.int32, (tq, LANES), 1)
    sub0 = lane < HEAD_DIM
    zero = jnp.zeros((tq, LANES), BF16)
    gain = sg_ref[...] * (1.0 - lambda_init)
    for i in range(seq // tq):
        q = q_ref[0, i * tq:(i + 1) * tq, :]
        q2 = jnp.concatenate([jnp.where(sub0, q, zero), jnp.where(sub0, zero, q)], axis=0)
        nk = (i + 1) * tq
        s = _dot_nt(q2, k_ref[0, 0:nk, :])
        row = lax.broadcasted_iota(jnp.int32, (2 * tq, nk), 0) % tq + i * tq
        col = lax.broadcasted_iota(jnp.int32, (2 * tq, nk), 1)
        s = jnp.where(col <= row, s, NEG_INF)
        m = jnp.max(s, axis=-1, keepdims=True)
        e = jnp.exp(s - m).astype(BF16)
        vv = v_ref[0, 0:nk, :]
        o = _dot(e, jnp.concatenate([vv, jnp.ones_like(vv)], axis=1))
        o = o[:, :LANES] / o[:, LANES:]
        w = o[:tq] - lam * o[tq:]
        o_ref[0, i * tq:(i + 1) * tq, :] = _rms(w, gain).astype(BF16)


def _diff_attention(proj, lam_params, subln, lambda_init):
    b, seq, _ = proj.shape
    q_off = 3 * A_WIDTH // LANES
    k_off = q_off + DIFF_QK_WIDTH // LANES
    v_off = k_off + DIFF_QK_WIDTH // LANES
    spec = lambda off: pl.BlockSpec((1, seq, LANES), lambda bi, hi: (bi, 0, off + hi))
    return pl.pallas_call(
        functools.partial(_diff_kernel, lambda_init=lambda_init),
        grid=(b, DIFF_HEADS),
        in_specs=[_resident(lam_params.shape), spec(q_off), spec(k_off), spec(v_off),
                  _resident(subln.shape)],
        out_specs=pl.BlockSpec((1, seq, LANES), lambda bi, hi: (bi, 0, hi)),
        out_shape=jax.ShapeDtypeStruct((b, seq, DIFF_V_WIDTH), BF16),
        compiler_params=_params(("parallel", "parallel")),
        name="diff_attention",
    )(lam_params, proj, proj, proj, subln)


def _rec_in_kernel(x_ref, g_ref, w_ref, lb_ref, o_ref, lf_ref, *, layer):
    h = _rms(x_ref[...], g_ref[...]).astype(BF16)
    lb_raw = lb_ref[...]
    lb_e = jnp.exp(lb_raw - jnp.max(lb_raw, axis=0, keepdims=True))
    lb_p = lb_e / jnp.sum(lb_e, axis=0, keepdims=True)
    lower = jnp.sum(lb_p[0:layer + 1], axis=0, keepdims=True) - lb_p[0:1]
    width = lf_ref.shape[1]
    n_cols = w_ref.shape[1]
    q_scale = HGRN_KEY_DIM ** -0.5
    for c0 in range(0, n_cols, COL_CHUNK):
        y = _dot(h, w_ref[:, c0:c0 + COL_CHUNK])
        group = c0 // width
        if group == 0:
            o_ref[:, c0:c0 + COL_CHUNK] = (y * _sigmoid(y) * q_scale).astype(BF16)
        elif group == 1:
            f0 = c0 - width
            lo = lower[:, f0:f0 + COL_CHUNK]
            forget = lo + (1.0 - lo) * _sigmoid(y)
            lf_ref[:, f0:f0 + COL_CHUNK] = jnp.log(forget)
            o_ref[:, c0:c0 + COL_CHUNK] = (1.0 - forget).astype(BF16)
        elif group == 2:
            o_ref[:, c0:c0 + COL_CHUNK] = y.astype(BF16)
        else:
            o_ref[:, c0:c0 + COL_CHUNK] = (y * _sigmoid(y)).astype(BF16)


def _rec_in_proj(x2d, gain, w, lower_bounds, layer, seq):
    t, d = x2d.shape
    n = w.shape[1]
    width = lower_bounds.shape[1]
    tm = min(ROW_TILE, seq)
    return pl.pallas_call(
        functools.partial(_rec_in_kernel, layer=layer),
        grid=(t // tm,),
        in_specs=[pl.BlockSpec((tm, d), lambda i: (i, 0)),
                  _resident((1, d)),
                  _resident((d, n)),
                  _resident(lower_bounds.shape)],
        out_specs=[pl.BlockSpec((tm, n), lambda i: (i, 0)),
                   pl.BlockSpec((tm, width), lambda i: (i, 0))],
        out_shape=[jax.ShapeDtypeStruct((t, n), BF16),
                   jax.ShapeDtypeStruct((t, width), F32)],
        compiler_params=_params(("parallel",)),
        name="rec_in_proj",
    )(x2d, gain, w, lower_bounds)


def _hgrn_kernel(q_ref, k_ref, v_ref, gate_ref, lf_ref, gn_ref, o_ref, st_ref):
    seq = q_ref.shape[1]
    ch = min(HGRN_CHUNK, seq)
    st_ref[...] = jnp.zeros_like(st_ref)
    gain = gn_ref[...]

    ti = lax.broadcasted_iota(jnp.int32, (ch, ch), 0)
    sj = lax.broadcasted_iota(jnp.int32, (ch, ch), 1)
    halves = [ch >> (i + 1) for i in range(ch.bit_length() - 1)]
    in_vreg = [h for h in halves if h < HGRN_ROW_GROUP]

    def ref_row(h):
        return (ti & -(2 * h)) + (h - 1)

    def prefix(idx):
        return jnp.where(sj <= idx, 1.0, 0.0)

    op = jnp.concatenate([prefix(ti)] + [prefix(ti) - prefix(ref_row(h)) for h in in_vreg],
                         axis=0).astype(BF16)
    level_mask = [jnp.where((((ti ^ sj) & -(2 * h)) == 0) & ((ti & h) != 0) & ((sj & h) == 0), 1.0, 0.0)
                  for h in halves]
    eye = jnp.where(ti == sj, 1.0, 0.0)

    group = min(HGRN_GROUP, seq // ch)

    def chunk_rows(gi, c):
        return pl.ds(pl.multiple_of(gi * (group * ch), ch) + c * ch, ch)

    def step(gi, carry):
        cums = []
        for c in range(group):
            g = lf_ref[0, chunk_rows(gi, c), :]
            g_hi = g.astype(BF16)
            g_lo = (g - g_hi.astype(F32)).astype(BF16)
            parts = _dot(op, jnp.concatenate([g_hi, g_lo], axis=1))
            cums.append(parts[:, :LANES] + parts[:, LANES:])

        attns, decays = [], []
        for c in range(group):
            rows = chunk_rows(gi, c)
            qb = q_ref[0, rows, :]
            kb = k_ref[0, rows, :]
            q = qb.astype(F32)
            k = kb.astype(F32)
            cum = cums[c]
            b = cum[:ch]
            b_last = b[ch - 1:ch, :]
            attn = _dot_nt(qb, kb) * eye
            for li, h in enumerate(halves):
                if h in in_vreg:
                    blk = 1 + in_vreg.index(h)
                    dist = cum[blk * ch:(blk + 1) * ch]
                else:
                    refs = [jnp.broadcast_to(b[a + h - 1:a + h, :], (2 * h, LANES))
                            for a in range(0, ch, 2 * h)]
                    dist = b - (jnp.concatenate(refs, axis=0) if len(refs) > 1 else refs[0])
                e = jnp.exp(-jnp.abs(dist))
                attn = attn + _dot_nt((q * e).astype(BF16), (k * e).astype(BF16)) * level_mask[li]
            attns.append(attn.astype(BF16))
            decays.append(((q * jnp.exp(b)).astype(BF16), (k * jnp.exp(b_last - b)).astype(BF16),
                           jnp.exp(b_last)))

        st = st_ref[...]
        for c in range(group):
            rows = chunk_rows(gi, c)
            v = v_ref[0, rows, :]
            q_dec, k_dec, total = decays[c]
            o = _dot_nt(q_dec, st.astype(BF16)) + _dot(attns[c], v)
            st = st * total + _dot_tn(v, k_dec)
            y = _rms(o, gain) * gate_ref[0, rows, :].astype(F32)
            o_ref[0, rows, :] = y.astype(BF16)
        st_ref[...] = st
        return carry

    lax.fori_loop(0, seq // (group * ch), step, 0)


def _hgrn_recurrence(proj, logf, gnorm):
    b, seq, _ = proj.shape
    h = HGRN_HEADS
    spec = lambda off: pl.BlockSpec((1, seq, LANES), lambda bi, hi: (bi, 0, off + hi))
    return pl.pallas_call(
        _hgrn_kernel,
        grid=(b, h),
        in_specs=[spec(0), spec(h), spec(2 * h), spec(3 * h), spec(0), _resident(gnorm.shape)],
        out_specs=pl.BlockSpec((1, seq, LANES), lambda bi, hi: (bi, 0, hi)),
        out_shape=jax.ShapeDtypeStruct((b, seq, h * HGRN_VAL_DIM), BF16),
        scratch_shapes=[pltpu.VMEM((HGRN_VAL_DIM, HGRN_KEY_DIM), F32)],
        compiler_params=_params(("parallel", "parallel")),
        name="hgrn_recurrence",
    )(proj, proj, proj, proj, logf, gnorm)


def _post_kernel(*refs, n_mix, hidden, chunks):
    mix_refs = refs[:n_mix]
    wo_ref, x_ref, g_ref, win_ref, wout_ref, o_ref = refs[n_mix:]
    gains = g_ref[...]
    mixed = jnp.concatenate([r[...] for r in mix_refs], axis=-1) if n_mix > 1 else mix_refs[0][...]
    x1 = x_ref[...] + _rms(_dot(mixed, wo_ref[...]), gains[1:2])
    h = _rms(x1, gains[2:3]).astype(BF16)
    acc = jnp.zeros(x1.shape, F32)
    for lo, hi in chunks:
        gate = _dot(h, win_ref[:, lo:hi])
        up = _dot(h, win_ref[:, hidden + lo:hidden + hi])
        act = (gate * _sigmoid(gate) * up).astype(BF16)
        acc = acc + _dot(act, wout_ref[lo:hi, :])
    o_ref[...] = x1 + _rms(acc, gains[3:4])


def _post_mixer(mix_list, w_out, x2d, gains, ffn_w_in, ffn_w_out, seq):
    t, d = x2d.shape
    hidden = ffn_w_out.shape[0]
    tm = min(ROW_TILE, seq)
    chunks = tuple((lo, min(lo + COL_CHUNK, hidden)) for lo in range(0, hidden, COL_CHUNK))
    mix_specs = [pl.BlockSpec((tm, m.shape[1]), lambda i: (i, 0)) for m in mix_list]
    return pl.pallas_call(
        functools.partial(_post_kernel, n_mix=len(mix_list), hidden=hidden, chunks=chunks),
        grid=(t // tm,),
        in_specs=mix_specs + [_resident(w_out.shape),
                              pl.BlockSpec((tm, d), lambda i: (i, 0)),
                              _resident(gains.shape),
                              _resident(ffn_w_in.shape),
                              _resident(ffn_w_out.shape)],
        out_specs=pl.BlockSpec((tm, d), lambda i: (i, 0)),
        out_shape=jax.ShapeDtypeStruct((t, d), F32),
        compiler_params=_params(("parallel",)),
        name="post_mixer_ffn",
    )(*mix_list, w_out, x2d, gains, ffn_w_in, ffn_w_out)


def _rope_tables(seq):
    half = HEAD_DIM // 2
    inv_freq = ROPE_THETA ** (-jnp.arange(0, HEAD_DIM, 2, dtype=F32) / HEAD_DIM)
    ang = jnp.arange(seq, dtype=F32)[:, None] * inv_freq[None, :]
    reps = LANES // half
    sign = jnp.tile(jnp.concatenate([-jnp.ones((half,), F32), jnp.ones((half,), F32)]), LANES // HEAD_DIM)
    return jnp.tile(jnp.cos(ang), (1, reps)), jnp.tile(jnp.sin(ang), (1, reps)) * sign[None, :]


def kernel(x, norm_gains, att_w_in, att_lambda, att_subln, att_w_out, rec_w_in, rec_lower_bounds,
           rec_gnorm, rec_w_out, ffn_w_in, ffn_w_out):
    b, seq, d = x.shape
    depth = norm_gains.shape[0]
    cos, sin = _rope_tables(seq)
    x2d = x.reshape(b * seq, d)
    for layer in range(depth):
        gains = norm_gains[layer]
        j = layer // 2
        if layer % 2 == 0:
            lambda_init = 0.8 - 0.6 * math.exp(-0.3 * layer)
            proj = _att_in_proj(x2d, gains[0:1], att_w_in[j].astype(BF16), cos, sin, seq)
            proj = proj.reshape(b, seq, -1)
            oa = _dilated_attention(proj)
            od = _diff_attention(proj, att_lambda[j], att_subln[j][None, :], lambda_init)
            mix = [oa.reshape(b * seq, -1), od.reshape(b * seq, -1)]
            w_out = att_w_out[j]
        else:
            proj, logf = _rec_in_proj(x2d, gains[0:1], rec_w_in[j].astype(BF16), rec_lower_bounds, j, seq)
            o = _hgrn_recurrence(proj.reshape(b, seq, -1), logf.reshape(b, seq, -1), rec_gnorm[j][None, :])
            mix = [o.reshape(b * seq, -1)]
            w_out = rec_w_out[j]
        x2d = _post_mixer(mix, w_out.astype(BF16), x2d, gains, ffn_w_in[layer].astype(BF16),
                          ffn_w_out[layer].astype(BF16), seq)
    return x2d.reshape(b, seq, d)
```

```python
import functools
import math

import jax
import jax.numpy as jnp
import numpy as np
from jax import lax
from jax.experimental import pallas as pl
from jax.experimental.pallas import tpu as pltpu

F32 = jnp.float32
BF16 = jnp.bfloat16

HEAD_DIM = 64
A_HEADS = 8
A_WIDTH = A_HEADS * HEAD_DIM
DILATED_PATTERNS = ((128, 1), (512, 4), (2048, 16))
DIL_BLOCK = 128
DIFF_HEADS = 4
DIFF_QK_WIDTH = 2 * DIFF_HEADS * HEAD_DIM
DIFF_V_DIM = 2 * HEAD_DIM
DIFF_V_WIDTH = DIFF_HEADS * DIFF_V_DIM
ROPE_THETA = 10000.0
HGRN_HEADS = 8
HGRN_KEY_DIM = 128
HGRN_VAL_DIM = 128
NORM_EPS = 1e-6
NEG_INF = -1e30

LANES = 128
V7X_VMEM_BYTES = 64 * 1024 * 1024
VMEM_LIMIT = V7X_VMEM_BYTES * 7 // 8

ROW_TILE = 512
COL_CHUNK = 512
DIL_GROUP = 4
DIFF_Q_TILE = 256
HGRN_CHUNK = 64
HGRN_GROUP = 4
HGRN_MIN_HALF_LOG2 = -96.0


def _rms(x, gain):
    ms = jnp.mean(x * x, axis=-1, keepdims=True)
    return x * lax.rsqrt(ms + NORM_EPS) * gain


def _sigmoid(x):
    return 1.0 / (1.0 + jnp.exp(-x))


def _dot(a, b):
    return jnp.dot(a, b, preferred_element_type=F32)


def _dot_nt(a, b):
    return lax.dot_general(a, b, (((1,), (1,)), ((), ())), preferred_element_type=F32)


def _dot_tn(a, b):
    return lax.dot_general(a, b, (((0,), (0,)), ((), ())), preferred_element_type=F32)


def _params(semantics):
    return pltpu.CompilerParams(dimension_semantics=semantics, vmem_limit_bytes=VMEM_LIMIT)


def _resident(shape):
    return pl.BlockSpec(shape, lambda *_: (0,) * len(shape), pipeline_mode=pl.Buffered(1))


def _att_in_kernel(x_ref, g_ref, w_ref, cos_ref, sin_ref, o_ref):
    h = _rms(x_ref[...], g_ref[...]).astype(BF16)
    tm = h.shape[0]
    cos = cos_ref[...]
    sin = sin_ref[...]
    lane = lax.broadcasted_iota(jnp.int32, (tm, LANES), 1)
    first_half = (lane % HEAD_DIM) < (HEAD_DIM // 2)
    n_cols = w_ref.shape[1]
    q_scale = HEAD_DIM ** -0.5
    rope_hi = 2 * A_WIDTH
    diff_lo = 3 * A_WIDTH
    diff_rope_hi = diff_lo + 2 * DIFF_QK_WIDTH
    for c0 in range(0, n_cols, COL_CHUNK):
        acc = _dot(h, w_ref[:, c0:c0 + COL_CHUNK])
        for l0 in range(0, COL_CHUNK, LANES):
            col = c0 + l0
            y = acc[:, l0:l0 + LANES]
            if col < rope_hi or diff_lo <= col < diff_rope_hi:
                rot = jnp.where(first_half, pltpu.roll(y, LANES - HEAD_DIM // 2, 1),
                                pltpu.roll(y, HEAD_DIM // 2, 1))
                y = y * cos + rot * sin
                if col < A_WIDTH or diff_lo <= col < diff_lo + DIFF_QK_WIDTH:
                    y = y * q_scale
            o_ref[:, col:col + LANES] = y.astype(BF16)


def _att_in_proj(x2d, gain, w, cos, sin, seq):
    t, d = x2d.shape
    n = w.shape[1]
    tm = min(ROW_TILE, seq)
    tiles_per_seq = seq // tm
    return pl.pallas_call(
        _att_in_kernel,
        grid=(t // tm,),
        in_specs=[pl.BlockSpec((tm, d), lambda i: (i, 0)),
                  _resident((1, d)),
                  _resident((d, n)),
                  pl.BlockSpec((tm, LANES), lambda i: (i % tiles_per_seq, 0)),
                  pl.BlockSpec((tm, LANES), lambda i: (i % tiles_per_seq, 0))],
        out_specs=pl.BlockSpec((tm, n), lambda i: (i, 0)),
        out_shape=jax.ShapeDtypeStruct((t, n), BF16),
        compiler_params=_params(("parallel",)),
        name="att_in_proj",
    )(x2d, gain, w, cos, sin)


def _dilated_kernel(q_ref, k_ref, v_ref, o_ref, qf, kf, vf, acc, mx, den):
    seq = q_ref.shape[1]
    blk = DIL_BLOCK
    qf[...] = q_ref[0].astype(F32)
    kf[...] = k_ref[0].astype(F32)
    vf[...] = v_ref[0].astype(F32)

    lane = lax.broadcasted_iota(jnp.int32, (blk, LANES), 1)
    head_a = lane < HEAD_DIM
    qi = lax.broadcasted_iota(jnp.int32, (2 * blk, 2 * blk), 0) % blk
    ci = lax.broadcasted_iota(jnp.int32, (2 * blk, 2 * blk), 1)
    mask_two = (ci >= qi) & (ci <= qi + blk)
    qi1 = lax.broadcasted_iota(jnp.int32, (2 * blk, blk), 0) % blk
    ci1 = lax.broadcasted_iota(jnp.int32, (2 * blk, blk), 1)
    mask_one = ci1 <= qi1

    zero = jnp.zeros((blk, LANES), BF16)

    def attend(p, blocks):
        scores = []
        for qb, kc, _, _, _ in blocks:
            q2 = jnp.concatenate([jnp.where(head_a, qb, zero), jnp.where(head_a, zero, qb)], axis=0)
            scores.append(_dot_nt(q2, kc))
        probs = []
        for s, (_, _, _, first, _) in zip(scores, blocks):
            s = jnp.where(mask_one if first else mask_two, s, NEG_INF)
            m = jnp.max(s, axis=-1, keepdims=True)
            probs.append((jnp.exp(s - m).astype(BF16), m))
        for (e, m), (_, _, vc, _, rows) in zip(probs, blocks):
            o = _dot(e, jnp.concatenate([vc, jnp.ones_like(vc)], axis=1))
            acc[p, rows, :] = jnp.where(head_a, o[:blk, :LANES], o[blk:, :LANES])
            mx[p, rows, :] = jnp.where(head_a, m[:blk], m[blk:])
            den[p, rows, :] = jnp.where(head_a, o[:blk, LANES:], o[blk:, LANES:])

    for p, (window, dil) in enumerate(DILATED_PATTERNS):
        assert window // dil == blk
        length = seq // dil
        nb = length // blk

        if dil == 1:
            for n0 in range(0, nb, DIL_GROUP):
                blocks = []
                for n in range(n0, min(n0 + DIL_GROUP, nb)):
                    keys = slice(max(n - 1, 0) * blk, (n + 1) * blk)
                    blocks.append((q_ref[0, n * blk:(n + 1) * blk, :], k_ref[0, keys, :], v_ref[0, keys, :],
                                   n == 0, pl.ds(n * blk, blk)))
                attend(p, blocks)
            continue

        per_step = max(DIL_GROUP // nb, 1)

        def residues(step, carry, p=p, dil=dil, length=length, nb=nb, per_step=per_step):
            blocks = []
            for j in range(per_step):
                r = step * per_step + j
                qr = qf[pl.ds(r, length, stride=dil), :].astype(BF16)
                kr = kf[pl.ds(r, length, stride=dil), :].astype(BF16)
                vr = vf[pl.ds(r, length, stride=dil), :].astype(BF16)
                for n in range(nb):
                    keys = slice(max(n - 1, 0) * blk, (n + 1) * blk)
                    blocks.append((qr[n * blk:(n + 1) * blk], kr[keys], vr[keys], n == 0,
                                   pl.ds(r + n * blk * dil, blk, stride=dil)))
            attend(p, blocks)
            return carry

        lax.fori_loop(0, dil // per_step, residues, 0)

    m_all = jnp.maximum(jnp.maximum(mx[0], mx[1]), mx[2])
    num = jnp.zeros((seq, LANES), F32)
    tot = jnp.zeros((seq, LANES), F32)
    for p in range(len(DILATED_PATTERNS)):
        w = jnp.exp(mx[p] - m_all)
        num = num + w * acc[p]
        tot = tot + w * den[p]
    o_ref[0] = (num / tot).astype(BF16)


def _dilated_attention(proj):
    b, seq, _ = proj.shape
    n_pairs = A_WIDTH // LANES
    spec = lambda off: pl.BlockSpec((1, seq, LANES), lambda bi, pi: (bi, 0, off + pi))
    n_pat = len(DILATED_PATTERNS)
    return pl.pallas_call(
        _dilated_kernel,
        grid=(b, n_pairs),
        in_specs=[spec(0), spec(n_pairs), spec(2 * n_pairs)],
        out_specs=pl.BlockSpec((1, seq, LANES), lambda bi, pi: (bi, 0, pi)),
        out_shape=jax.ShapeDtypeStruct((b, seq, A_WIDTH), BF16),
        scratch_shapes=[pltpu.VMEM((seq, LANES), F32)] * 3
        + [pltpu.VMEM((n_pat, seq, LANES), F32)] * 3,
        compiler_params=_params(("parallel", "parallel")),
        name="dilated_attention",
    )(proj, proj, proj)


def _diff_kernel(lam_ref, q_ref, k_ref, v_ref, sg_ref, o_ref, *, lambda_init):
    lp = lam_ref[...]
    l1 = jnp.sum(lp[0:1] * lp[1:2], axis=-1, keepdims=True)
    l2 = jnp.sum(lp[2:3] * lp[3:4], axis=-1, keepdims=True)
    lam = jnp.exp(l1) - jnp.exp(l2) + lambda_init
    seq = q_ref.shape[1]
    tq = min(DIFF_Q_TILE, seq)
    lane = lax.broadcasted_iota(jnp.int32, (tq, LANES), 1)
    sub0 = lane < HEAD_DIM
    zero = jnp.zeros((tq, LANES), BF16)
    gain = sg_ref[...] * (1.0 - lambda_init)
    for i in range(seq // tq):
        q = q_ref[0, i * tq:(i + 1) * tq, :]
        q2 = jnp.concatenate([jnp.where(sub0, q, zero), jnp.where(sub0, zero, q)], axis=0)
        nk = (i + 1) * tq
        s = _dot_nt(q2, k_ref[0, 0:nk, :])
        row = lax.broadcasted_iota(jnp.int32, (2 * tq, nk), 0) % tq + i * tq
        col = lax.broadcasted_iota(jnp.int32, (2 * tq, nk), 1)
        s = jnp.where(col <= row, s, NEG_INF)
        m = jnp.max(s, axis=-1, keepdims=True)
        e = jnp.exp(s - m).astype(BF16)
        vv = v_ref[0, 0:nk, :]
        o = _dot(e, jnp.concatenate([vv, jnp.ones_like(vv)], axis=1))
        o = o[:, :LANES] / o[:, LANES:]
        w = o[:tq] - lam * o[tq:]
        o_ref[0, i * tq:(i + 1) * tq, :] = _rms(w, gain).astype(BF16)


def _diff_attention(proj, lam_params, subln, lambda_init):
    b, seq, _ = proj.shape
    q_off = 3 * A_WIDTH // LANES
    k_off = q_off + DIFF_QK_WIDTH // LANES
    v_off = k_off + DIFF_QK_WIDTH // LANES
    spec = lambda off: pl.BlockSpec((1, seq, LANES), lambda bi, hi: (bi, 0, off + hi))
    return pl.pallas_call(
        functools.partial(_diff_kernel, lambda_init=lambda_init),
        grid=(b, DIFF_HEADS),
        in_specs=[_resident(lam_params.shape), spec(q_off), spec(k_off), spec(v_off),
                  _resident(subln.shape)],
        out_specs=pl.BlockSpec((1, seq, LANES), lambda bi, hi: (bi, 0, hi)),
        out_shape=jax.ShapeDtypeStruct((b, seq, DIFF_V_WIDTH), BF16),
        compiler_params=_params(("parallel", "parallel")),
        name="diff_attention",
    )(lam_params, proj, proj, proj, subln)


def _rec_in_kernel(x_ref, g_ref, w_ref, lb_ref, o_ref, lf_ref, *, layer):
    h = _rms(x_ref[...], g_ref[...]).astype(BF16)
    lb_raw = lb_ref[...]
    lb_e = jnp.exp(lb_raw - jnp.max(lb_raw, axis=0, keepdims=True))
    lb_p = lb_e / jnp.sum(lb_e, axis=0, keepdims=True)
    lower = jnp.sum(lb_p[0:layer + 1], axis=0, keepdims=True) - lb_p[0:1]
    width = lf_ref.shape[1]
    n_cols = w_ref.shape[1]
    q_scale = HGRN_KEY_DIM ** -0.5
    for c0 in range(0, n_cols, COL_CHUNK):
        y = _dot(h, w_ref[:, c0:c0 + COL_CHUNK])
        group = c0 // width
        if group == 0:
            o_ref[:, c0:c0 + COL_CHUNK] = (y * _sigmoid(y) * q_scale).astype(BF16)
        elif group == 1:
            f0 = c0 - width
            lo = lower[:, f0:f0 + COL_CHUNK]
            forget = lo + (1.0 - lo) * _sigmoid(y)
            lf_ref[:, f0:f0 + COL_CHUNK] = jnp.log2(forget)
            o_ref[:, c0:c0 + COL_CHUNK] = (1.0 - forget).astype(BF16)
        elif group == 2:
            o_ref[:, c0:c0 + COL_CHUNK] = y.astype(BF16)
        else:
            o_ref[:, c0:c0 + COL_CHUNK] = (y * _sigmoid(y)).astype(BF16)


def _rec_in_proj(x2d, gain, w, lower_bounds, layer, seq):
    t, d = x2d.shape
    n = w.shape[1]
    width = lower_bounds.shape[1]
    tm = min(ROW_TILE, seq)
    return pl.pallas_call(
        functools.partial(_rec_in_kernel, layer=layer),
        grid=(t // tm,),
        in_specs=[pl.BlockSpec((tm, d), lambda i: (i, 0)),
                  _resident((1, d)),
                  _resident((d, n)),
                  _resident(lower_bounds.shape)],
        out_specs=[pl.BlockSpec((tm, n), lambda i: (i, 0)),
                   pl.BlockSpec((tm, width), lambda i: (i, 0))],
        out_shape=[jax.ShapeDtypeStruct((t, n), BF16),
                   jax.ShapeDtypeStruct((t, width), F32)],
        compiler_params=_params(("parallel",)),
        name="rec_in_proj",
    )(x2d, gain, w, lower_bounds)


def _hgrn_levels():
    return [HGRN_CHUNK >> (i + 1) for i in range(HGRN_CHUNK.bit_length() - 1)]


def _hgrn_tables():
    ch = HGRN_CHUNK
    pair = 2 * ch
    t = np.arange(pair)
    col = np.arange(pair)[None, :]
    same_chunk = (col // ch) == (t[:, None] // ch)

    def prefix(idx):
        return (same_chunk & (col <= idx[:, None])).astype(np.float32)

    last = (t // ch) * ch + ch - 1
    blocks = [prefix(t), prefix(last) - prefix(t)]
    levels = _hgrn_levels()
    for h in levels:
        if h == 1:
            continue
        ref = (t & -(2 * h)) + h - 1
        sign = np.where((t & h) != 0, 1.0, -1.0)[:, None]
        blocks.append(sign * (prefix(t) - prefix(ref)))
    op = np.concatenate(blocks, axis=0)
    op = np.concatenate([op, op], axis=1)

    ti = np.arange(ch)[:, None]
    sj = np.arange(pair)[None, :] & (ch - 1)
    masks = [(ti == sj)]
    for h in levels:
        masks.append((((ti ^ sj) & -(2 * h)) == 0) & ((ti & h) != 0) & ((sj & h) == 0))
    masks.append((((ti ^ sj) & -(ch // 2)) == 0) & (sj <= ti))
    return jnp.asarray(op, BF16), jnp.asarray(np.stack(masks), F32)


def _hgrn_kernel(op_ref, mask_ref, q_ref, k_ref, v_ref, gate_ref, lf_ref, gn_ref, o_ref, st_ref):
    seq = q_ref.shape[1]
    ch = HGRN_CHUNK
    pair = 2 * ch
    st_ref[...] = jnp.zeros_like(st_ref)
    gain = gn_ref[...]
    halves = _hgrn_levels()
    first_chunk = lax.broadcasted_iota(jnp.int32, (ch, pair), 1) < ch
    row_id = lax.broadcasted_iota(jnp.int32, (pair, LANES), 0)
    odd_row = (row_id & 1) == 1
    first_half = (row_id & (ch // 2)) == 0
    zeros_half = jnp.zeros((ch, LANES), BF16)

    def side_by_side(x):
        return jnp.concatenate([x[:ch], x[ch:]], axis=1)

    def block_diag(x):
        return jnp.concatenate([jnp.concatenate([x[:ch], zeros_half], axis=1),
                                jnp.concatenate([zeros_half, x[ch:]], axis=1)], axis=0)

    group = min(HGRN_GROUP, seq // pair)

    def pair_rows(gi, c):
        return pl.ds(pl.multiple_of(gi * (group * pair), pair) + c * pair, pair)

    def chunk_bcast(b, row):
        return jnp.concatenate([jnp.broadcast_to(b[c0 + row:c0 + row + 1, :], (ch, LANES))
                                for c0 in (0, ch)], axis=0)

    def scores_any_decay(qb, kb, cum, g):
        attn = _dot_nt(side_by_side(qb), block_diag(kb)) * mask_ref[0]
        for li, h in enumerate(halves):
            if h == 1:
                e = jnp.where(odd_row, jnp.exp2(g), 1.0).astype(BF16)
            else:
                e = jnp.exp2(cum[(2 + li) * pair:(3 + li) * pair]).astype(BF16)
            attn = attn + _dot_nt(side_by_side(qb * e), block_diag(kb * e)) * mask_ref[1 + li]
        b = cum[:pair]
        return attn, qb * jnp.exp2(b).astype(BF16), kb * jnp.exp2(cum[pair:2 * pair]).astype(BF16)

    def scores_bounded_decay(qb, kb, b):
        mid = ch // 2
        d = b - chunk_bcast(b, mid - 1)
        e = jnp.exp2(-jnp.abs(d)).astype(BF16)
        q_top = qb * e
        attn = _dot_nt(side_by_side(q_top), block_diag(kb * e)) * mask_ref[1]
        q_dec = qb * jnp.exp2(b).astype(BF16)
        q_loc = jnp.where(first_half, q_dec, q_top)
        k_loc = kb * jnp.exp2(jnp.where(first_half, -b, -d)).astype(BF16)
        attn = attn + _dot_nt(side_by_side(q_loc), block_diag(k_loc)) * mask_ref[len(halves) + 1]
        k_dec = kb * jnp.exp2(chunk_bcast(b, ch - 1) - b).astype(BF16)
        return attn, q_dec, k_dec

    def step(gi, carry, bounded):
        op = op_ref[:pair, :] if bounded else op_ref[...]
        logs, cums = [], []
        for c in range(group):
            g = lf_ref[0, pair_rows(gi, c), :]
            g_hi = g.astype(BF16)
            g_lo = (g - g_hi.astype(F32)).astype(BF16)
            logs.append(g)
            cums.append(_dot(op, jnp.concatenate([g_hi, g_lo], axis=0)))

        lhs, decays = [], []
        for c in range(group):
            rows = pair_rows(gi, c)
            qb = q_ref[0, rows, :]
            kb = k_ref[0, rows, :]
            b = cums[c][:pair]
            if bounded:
                attn, q_dec, k_dec = scores_bounded_decay(qb, kb, b)
            else:
                attn, q_dec, k_dec = scores_any_decay(qb, kb, cums[c], logs[c])
            cross = _dot_nt(q_dec[ch:], k_dec)
            lhs.append(jnp.concatenate([jnp.where(first_chunk, attn, 0.0),
                                        jnp.where(first_chunk, cross, attn)], axis=0).astype(BF16))
            tot0 = jnp.exp2(b[ch - 1:ch, :])
            tot1 = jnp.exp2(b[pair - 1:pair, :])
            q_in = jnp.concatenate([q_dec[:ch], q_dec[ch:] * tot0.astype(BF16)], axis=0)
            k_up = jnp.concatenate([k_dec[:ch] * tot1.astype(BF16), k_dec[ch:]], axis=0)
            decays.append((q_in, k_up, tot0 * tot1))

        st = st_ref[...]
        for c in range(group):
            rows = pair_rows(gi, c)
            v = v_ref[0, rows, :]
            q_in, k_up, total = decays[c]
            o = _dot_nt(q_in, st.astype(BF16)) + _dot(lhs[c], v)
            st = st * total + _dot_tn(v, k_up)
            y = _rms(o, gain) * gate_ref[0, rows, :].astype(F32)
            o_ref[0, rows, :] = y.astype(BF16)
        st_ref[...] = st
        return carry

    half_rows = ch // 2
    half_tot = jnp.sum(lf_ref[0].reshape(seq // half_rows, half_rows, LANES), axis=1)
    worst = jnp.min(jnp.min(half_tot, axis=0, keepdims=True), axis=1, keepdims=True)
    bounded_ok = worst[0, 0] >= HGRN_MIN_HALF_LOG2
    n_steps = seq // (group * pair)

    @pl.when(bounded_ok)
    def _():
        lax.fori_loop(0, n_steps, functools.partial(step, bounded=True), 0)

    @pl.when(jnp.logical_not(bounded_ok))
    def _():
        lax.fori_loop(0, n_steps, functools.partial(step, bounded=False), 0)


def _hgrn_recurrence(proj, logf, gnorm):
    b, seq, _ = proj.shape
    h = HGRN_HEADS
    spec = lambda off: pl.BlockSpec((1, seq, LANES), lambda bi, hi: (bi, 0, off + hi))
    op, masks = _hgrn_tables()
    return pl.pallas_call(
        _hgrn_kernel,
        grid=(b, h),
        in_specs=[_resident(op.shape), _resident(masks.shape),
                  spec(0), spec(h), spec(2 * h), spec(3 * h), spec(0), _resident(gnorm.shape)],
        out_specs=pl.BlockSpec((1, seq, LANES), lambda bi, hi: (bi, 0, hi)),
        out_shape=jax.ShapeDtypeStruct((b, seq, h * HGRN_VAL_DIM), BF16),
        scratch_shapes=[pltpu.VMEM((HGRN_VAL_DIM, HGRN_KEY_DIM), F32)],
        compiler_params=_params(("parallel", "parallel")),
        name="hgrn_recurrence",
    )(op, masks, proj, proj, proj, proj, logf, gnorm)


def _post_kernel(*refs, n_mix, hidden, chunks):
    mix_refs = refs[:n_mix]
    wo_ref, x_ref, g_ref, win_ref, wout_ref, o_ref = refs[n_mix:]
    gains = g_ref[...]
    mixed = jnp.concatenate([r[...] for r in mix_refs], axis=-1) if n_mix > 1 else mix_refs[0][...]
    x1 = x_ref[...] + _rms(_dot(mixed, wo_ref[...]), gains[1:2])
    h = _rms(x1, gains[2:3]).astype(BF16)
    acc = jnp.zeros(x1.shape, F32)
    for lo, hi in chunks:
        gate = _dot(h, win_ref[:, lo:hi])
        up = _dot(h, win_ref[:, hidden + lo:hidden + hi])
        act = (gate * _sigmoid(gate) * up).astype(BF16)
        acc = acc + _dot(act, wout_ref[lo:hi, :])
    o_ref[...] = x1 + _rms(acc, gains[3:4])


def _post_mixer(mix_list, w_out, x2d, gains, ffn_w_in, ffn_w_out, seq):
    t, d = x2d.shape
    hidden = ffn_w_out.shape[0]
    tm = min(ROW_TILE, seq)
    chunks = tuple((lo, min(lo + COL_CHUNK, hidden)) for lo in range(0, hidden, COL_CHUNK))
    mix_specs = [pl.BlockSpec((tm, m.shape[1]), lambda i: (i, 0)) for m in mix_list]
    return pl.pallas_call(
        functools.partial(_post_kernel, n_mix=len(mix_list), hidden=hidden, chunks=chunks),
        grid=(t // tm,),
        in_specs=mix_specs + [_resident(w_out.shape),
                              pl.BlockSpec((tm, d), lambda i: (i, 0)),
                              _resident(gains.shape),
                              _resident(ffn_w_in.shape),
                              _resident(ffn_w_out.shape)],
        out_specs=pl.BlockSpec((tm, d), lambda i: (i, 0)),
        out_shape=jax.ShapeDtypeStruct((t, d), F32),
        compiler_params=_params(("parallel",)),
        name="post_mixer_ffn",
    )(*mix_list, w_out, x2d, gains, ffn_w_in, ffn_w_out)


def _rope_tables(seq):
    half = HEAD_DIM // 2
    inv_freq = ROPE_THETA ** (-jnp.arange(0, HEAD_DIM, 2, dtype=F32) / HEAD_DIM)
    ang = jnp.arange(seq, dtype=F32)[:, None] * inv_freq[None, :]
    reps = LANES // half
    sign = jnp.tile(jnp.concatenate([-jnp.ones((half,), F32), jnp.ones((half,), F32)]), LANES // HEAD_DIM)
    return jnp.tile(jnp.cos(ang), (1, reps)), jnp.tile(jnp.sin(ang), (1, reps)) * sign[None, :]


def kernel(x, norm_gains, att_w_in, att_lambda, att_subln, att_w_out, rec_w_in, rec_lower_bounds,
           rec_gnorm, rec_w_out, ffn_w_in, ffn_w_out):
    b, seq, d = x.shape
    depth = norm_gains.shape[0]
    cos, sin = _rope_tables(seq)
    x2d = x.reshape(b * seq, d)
    for layer in range(depth):
        gains = norm_gains[layer]
        j = layer // 2
        if layer % 2 == 0:
            lambda_init = 0.8 - 0.6 * math.exp(-0.3 * layer)
            proj = _att_in_proj(x2d, gains[0:1], att_w_in[j].astype(BF16), cos, sin, seq)
            proj = proj.reshape(b, seq, -1)
            oa = _dilated_attention(proj)
            od = _diff_attention(proj, att_lambda[j], att_subln[j][None, :], lambda_init)
            mix = [oa.reshape(b * seq, -1), od.reshape(b * seq, -1)]
            w_out = att_w_out[j]
        else:
            proj, logf = _rec_in_proj(x2d, gains[0:1], rec_w_in[j].astype(BF16), rec_lower_bounds, j, seq)
            o = _hgrn_recurrence(proj.reshape(b, seq, -1), logf.reshape(b, seq, -1), rec_gnorm[j][None, :])
            mix = [o.reshape(b * seq, -1)]
            w_out = rec_w_out[j]
        x2d = _post_mixer(mix, w_out.astype(BF16), x2d, gains, ffn_w_in[layer].astype(BF16),
                          ffn_w_out[layer].astype(BF16), seq)
    return x2d.reshape(b, seq, d)
```

```python
import functools
import math

import jax
import jax.numpy as jnp
import numpy as np
from jax import lax
from jax.experimental import pallas as pl
from jax.experimental.pallas import tpu as pltpu

F32 = jnp.float32
BF16 = jnp.bfloat16

HEAD_DIM = 64
A_HEADS = 8
A_WIDTH = A_HEADS * HEAD_DIM
DILATED_PATTERNS = ((128, 1), (512, 4), (2048, 16))
DIL_BLOCK = 128
DIFF_HEADS = 4
DIFF_QK_WIDTH = 2 * DIFF_HEADS * HEAD_DIM
DIFF_V_DIM = 2 * HEAD_DIM
DIFF_V_WIDTH = DIFF_HEADS * DIFF_V_DIM
ROPE_THETA = 10000.0
HGRN_HEADS = 8
HGRN_KEY_DIM = 128
HGRN_VAL_DIM = 128
NORM_EPS = 1e-6
NEG_INF = -1e30
LOG2_E = math.log2(math.e)

LANES = 128
V7X_VMEM_BYTES = 64 * 1024 * 1024
VMEM_LIMIT = V7X_VMEM_BYTES * 7 // 8

ROW_TILE = 512
COL_CHUNK = 512
DIL_GROUP = 4
DIFF_Q_TILE = 256
HGRN_CHUNK = 64
HGRN_GROUP = 4
HGRN_MIN_HALF_LOG2 = -96.0


def _rms(x, gain):
    ms = jnp.mean(x * x, axis=-1, keepdims=True)
    return x * lax.rsqrt(ms + NORM_EPS) * gain


def _sigmoid(x):
    return 1.0 / (1.0 + jnp.exp(-x))


def _dot(a, b):
    return jnp.dot(a, b, preferred_element_type=F32)


def _dot_nt(a, b):
    return lax.dot_general(a, b, (((1,), (1,)), ((), ())), preferred_element_type=F32)


def _dot_tn(a, b):
    return lax.dot_general(a, b, (((0,), (0,)), ((), ())), preferred_element_type=F32)


def _params(semantics):
    return pltpu.CompilerParams(dimension_semantics=semantics, vmem_limit_bytes=VMEM_LIMIT)


def _resident(shape):
    return pl.BlockSpec(shape, lambda *_: (0,) * len(shape), pipeline_mode=pl.Buffered(1))


def _layer_block(stack, layer):
    tail = stack.shape[1:]
    return pl.BlockSpec((None,) + tail, lambda *_: (layer,) + (0,) * len(tail), pipeline_mode=pl.Buffered(1))


def _att_in_kernel(x_ref, g_ref, w_ref, cos_ref, sin_ref, o_ref):
    h = _rms(x_ref[...], g_ref[...]).astype(BF16)
    tm = h.shape[0]
    cos = cos_ref[...]
    sin = sin_ref[...]
    lane = lax.broadcasted_iota(jnp.int32, (tm, LANES), 1)
    first_half = (lane % HEAD_DIM) < (HEAD_DIM // 2)
    n_cols = w_ref.shape[1]
    q_scale = HEAD_DIM ** -0.5 * LOG2_E
    rope_hi = 2 * A_WIDTH
    diff_lo = 3 * A_WIDTH
    diff_rope_hi = diff_lo + 2 * DIFF_QK_WIDTH
    for c0 in range(0, n_cols, COL_CHUNK):
        acc = _dot(h, w_ref[:, c0:c0 + COL_CHUNK])
        for l0 in range(0, COL_CHUNK, LANES):
            col = c0 + l0
            y = acc[:, l0:l0 + LANES]
            if col < rope_hi or diff_lo <= col < diff_rope_hi:
                rot = jnp.where(first_half, pltpu.roll(y, LANES - HEAD_DIM // 2, 1),
                                pltpu.roll(y, HEAD_DIM // 2, 1))
                y = y * cos + rot * sin
                if col < A_WIDTH or diff_lo <= col < diff_lo + DIFF_QK_WIDTH:
                    y = y * q_scale
            o_ref[:, col:col + LANES] = y.astype(BF16)


def _att_in_proj(x2d, gain, w_stack, layer, cos, sin, seq):
    t, d = x2d.shape
    n = w_stack.shape[2]
    tm = min(ROW_TILE, seq)
    tiles_per_seq = seq // tm
    return pl.pallas_call(
        _att_in_kernel,
        grid=(t // tm,),
        in_specs=[pl.BlockSpec((tm, d), lambda i: (i, 0)),
                  _resident((1, d)),
                  _layer_block(w_stack, layer),
                  pl.BlockSpec((tm, LANES), lambda i: (i % tiles_per_seq, 0)),
                  pl.BlockSpec((tm, LANES), lambda i: (i % tiles_per_seq, 0))],
        out_specs=pl.BlockSpec((tm, n), lambda i: (i, 0)),
        out_shape=jax.ShapeDtypeStruct((t, n), BF16),
        compiler_params=_params(("parallel",)),
        name="att_in_proj",
    )(x2d, gain, w_stack, cos, sin)


def _dilated_kernel(q_ref, k_ref, v_ref, o_ref, qf, kf, vf, acc, mx, den):
    seq = q_ref.shape[1]
    blk = DIL_BLOCK
    qf[...] = q_ref[0].astype(F32)
    kf[...] = k_ref[0].astype(F32)
    vf[...] = v_ref[0].astype(F32)

    lane = lax.broadcasted_iota(jnp.int32, (blk, LANES), 1)
    head_a = lane < HEAD_DIM
    qi = lax.broadcasted_iota(jnp.int32, (2 * blk, 2 * blk), 0) % blk
    ci = lax.broadcasted_iota(jnp.int32, (2 * blk, 2 * blk), 1)
    mask_two = (ci >= qi) & (ci <= qi + blk)
    qi1 = lax.broadcasted_iota(jnp.int32, (2 * blk, blk), 0) % blk
    ci1 = lax.broadcasted_iota(jnp.int32, (2 * blk, blk), 1)
    mask_one = ci1 <= qi1

    zero = jnp.zeros((blk, LANES), BF16)

    def attend(p, blocks):
        scores = []
        for qb, kc, _, _, _ in blocks:
            q2 = jnp.concatenate([jnp.where(head_a, qb, zero), jnp.where(head_a, zero, qb)], axis=0)
            scores.append(_dot_nt(q2, kc))
        probs = []
        for s, (_, _, _, first, _) in zip(scores, blocks):
            s = jnp.where(mask_one if first else mask_two, s, NEG_INF)
            m = jnp.max(s, axis=-1, keepdims=True)
            probs.append((jnp.exp2(s - m).astype(BF16), m))
        for (e, m), (_, _, vc, _, rows) in zip(probs, blocks):
            o = _dot(e, jnp.concatenate([vc, jnp.ones_like(vc)], axis=1))
            acc[p, rows, :] = jnp.where(head_a, o[:blk, :LANES], o[blk:, :LANES])
            mx[p, rows, :] = jnp.where(head_a, m[:blk], m[blk:])
            den[p, rows, :] = jnp.where(head_a, o[:blk, LANES:], o[blk:, LANES:])

    for p, (window, dil) in enumerate(DILATED_PATTERNS):
        assert window // dil == blk
        length = seq // dil
        nb = length // blk

        if dil == 1:
            for n0 in range(0, nb, DIL_GROUP):
                blocks = []
                for n in range(n0, min(n0 + DIL_GROUP, nb)):
                    keys = slice(max(n - 1, 0) * blk, (n + 1) * blk)
                    blocks.append((q_ref[0, n * blk:(n + 1) * blk, :], k_ref[0, keys, :], v_ref[0, keys, :],
                                   n == 0, pl.ds(n * blk, blk)))
                attend(p, blocks)
            continue

        per_step = max(DIL_GROUP // nb, 1)

        def residues(step, carry, p=p, dil=dil, length=length, nb=nb, per_step=per_step):
            blocks = []
            for j in range(per_step):
                r = step * per_step + j
                qr = qf[pl.ds(r, length, stride=dil), :].astype(BF16)
                kr = kf[pl.ds(r, length, stride=dil), :].astype(BF16)
                vr = vf[pl.ds(r, length, stride=dil), :].astype(BF16)
                for n in range(nb):
                    keys = slice(max(n - 1, 0) * blk, (n + 1) * blk)
                    blocks.append((qr[n * blk:(n + 1) * blk], kr[keys], vr[keys], n == 0,
                                   pl.ds(r + n * blk * dil, blk, stride=dil)))
            attend(p, blocks)
            return carry

        lax.fori_loop(0, dil // per_step, residues, 0)

    m_all = jnp.maximum(jnp.maximum(mx[0], mx[1]), mx[2])
    num = jnp.zeros((seq, LANES), F32)
    tot = jnp.zeros((seq, LANES), F32)
    for p in range(len(DILATED_PATTERNS)):
        w = jnp.exp2(mx[p] - m_all)
        num = num + w * acc[p]
        tot = tot + w * den[p]
    o_ref[0] = (num / tot).astype(BF16)


def _dilated_attention(proj):
    b, seq, _ = proj.shape
    n_pairs = A_WIDTH // LANES
    spec = lambda off: pl.BlockSpec((1, seq, LANES), lambda bi, pi: (bi, 0, off + pi))
    n_pat = len(DILATED_PATTERNS)
    return pl.pallas_call(
        _dilated_kernel,
        grid=(b, n_pairs),
        in_specs=[spec(0), spec(n_pairs), spec(2 * n_pairs)],
        out_specs=pl.BlockSpec((1, seq, LANES), lambda bi, pi: (bi, 0, pi)),
        out_shape=jax.ShapeDtypeStruct((b, seq, A_WIDTH), BF16),
        scratch_shapes=[pltpu.VMEM((seq, LANES), F32)] * 3
        + [pltpu.VMEM((n_pat, seq, LANES), F32)] * 3,
        compiler_params=_params(("parallel", "parallel")),
        name="dilated_attention",
    )(proj, proj, proj)


def _diff_kernel(lam_ref, q_ref, k_ref, v_ref, sg_ref, o_ref, *, lambda_init):
    lp = lam_ref[...]
    l1 = jnp.sum(lp[0:1] * lp[1:2], axis=-1, keepdims=True)
    l2 = jnp.sum(lp[2:3] * lp[3:4], axis=-1, keepdims=True)
    lam = jnp.exp(l1) - jnp.exp(l2) + lambda_init
    seq = q_ref.shape[1]
    tq = min(DIFF_Q_TILE, seq)
    lane = lax.broadcasted_iota(jnp.int32, (tq, LANES), 1)
    sub0 = lane < HEAD_DIM
    zero = jnp.zeros((tq, LANES), BF16)
    gain = sg_ref[...] * (1.0 - lambda_init)
    causal = (lax.broadcasted_iota(jnp.int32, (2 * tq, tq), 1)
              <= lax.broadcasted_iota(jnp.int32, (2 * tq, tq), 0) % tq)

    def with_ones(vv):
        return jnp.concatenate([vv, jnp.ones_like(vv)], axis=1)

    for i in range(seq // tq):
        q = q_ref[0, i * tq:(i + 1) * tq, :]
        q2 = jnp.concatenate([jnp.where(sub0, q, zero), jnp.where(sub0, zero, q)], axis=0)
        past = i * tq
        s_diag = jnp.where(causal, _dot_nt(q2, k_ref[0, past:past + tq, :]), NEG_INF)
        m = jnp.max(s_diag, axis=-1, keepdims=True)
        if past:
            s_past = _dot_nt(q2, k_ref[0, 0:past, :])
            m = jnp.maximum(m, jnp.max(s_past, axis=-1, keepdims=True))
        o = _dot(jnp.exp2(s_diag - m).astype(BF16), with_ones(v_ref[0, past:past + tq, :]))
        if past:
            o = o + _dot(jnp.exp2(s_past - m).astype(BF16), with_ones(v_ref[0, 0:past, :]))
        o = o[:, :LANES] / o[:, LANES:]
        w = o[:tq] - lam * o[tq:]
        o_ref[0, i * tq:(i + 1) * tq, :] = _rms(w, gain).astype(BF16)


def _diff_attention(proj, lam_params, subln, lambda_init):
    b, seq, _ = proj.shape
    q_off = 3 * A_WIDTH // LANES
    k_off = q_off + DIFF_QK_WIDTH // LANES
    v_off = k_off + DIFF_QK_WIDTH // LANES
    spec = lambda off: pl.BlockSpec((1, seq, LANES), lambda bi, hi: (bi, 0, off + hi))
    return pl.pallas_call(
        functools.partial(_diff_kernel, lambda_init=lambda_init),
        grid=(b, DIFF_HEADS),
        in_specs=[_resident(lam_params.shape), spec(q_off), spec(k_off), spec(v_off),
                  _resident(subln.shape)],
        out_specs=pl.BlockSpec((1, seq, LANES), lambda bi, hi: (bi, 0, hi)),
        out_shape=jax.ShapeDtypeStruct((b, seq, DIFF_V_WIDTH), BF16),
        compiler_params=_params(("parallel", "parallel")),
        name="diff_attention",
    )(lam_params, proj, proj, proj, subln)


def _rec_in_kernel(x_ref, g_ref, w_ref, lb_ref, o_ref, lf_ref, *, layer):
    h = _rms(x_ref[...], g_ref[...]).astype(BF16)
    lb_raw = lb_ref[...]
    lb_e = jnp.exp(lb_raw - jnp.max(lb_raw, axis=0, keepdims=True))
    lb_p = lb_e / jnp.sum(lb_e, axis=0, keepdims=True)
    lower = jnp.sum(lb_p[0:layer + 1], axis=0, keepdims=True) - lb_p[0:1]
    width = lf_ref.shape[1]
    n_cols = w_ref.shape[1]
    q_scale = HGRN_KEY_DIM ** -0.5
    starts = sorted(range(0, n_cols, COL_CHUNK), key=lambda c: (c // width == 2, c))
    for c0 in starts:
        y = _dot(h, w_ref[:, c0:c0 + COL_CHUNK])
        group = c0 // width
        if group == 0:
            o_ref[:, c0:c0 + COL_CHUNK] = (y * _sigmoid(y) * q_scale).astype(BF16)
        elif group == 1:
            f0 = c0 - width
            lo = lower[:, f0:f0 + COL_CHUNK]
            forget = lo + (1.0 - lo) * _sigmoid(y)
            lf_ref[:, f0:f0 + COL_CHUNK] = jnp.log2(forget)
            o_ref[:, c0:c0 + COL_CHUNK] = (1.0 - forget).astype(BF16)
        elif group == 2:
            o_ref[:, c0:c0 + COL_CHUNK] = y.astype(BF16)
        else:
            o_ref[:, c0:c0 + COL_CHUNK] = (y * _sigmoid(y)).astype(BF16)


def _rec_in_proj(x2d, gain, w_stack, lower_bounds, layer, seq):
    t, d = x2d.shape
    n = w_stack.shape[2]
    width = lower_bounds.shape[1]
    tm = min(ROW_TILE, seq)
    return pl.pallas_call(
        functools.partial(_rec_in_kernel, layer=layer),
        grid=(t // tm,),
        in_specs=[pl.BlockSpec((tm, d), lambda i: (i, 0)),
                  _resident((1, d)),
                  _layer_block(w_stack, layer),
                  _resident(lower_bounds.shape)],
        out_specs=[pl.BlockSpec((tm, n), lambda i: (i, 0)),
                   pl.BlockSpec((tm, width), lambda i: (i, 0))],
        out_shape=[jax.ShapeDtypeStruct((t, n), BF16),
                   jax.ShapeDtypeStruct((t, width), F32)],
        compiler_params=_params(("parallel",)),
        name="rec_in_proj",
    )(x2d, gain, w_stack, lower_bounds)


def _hgrn_levels():
    return [HGRN_CHUNK >> (i + 1) for i in range(HGRN_CHUNK.bit_length() - 1)]


def _hgrn_tables():
    ch = HGRN_CHUNK
    pair = 2 * ch
    t = np.arange(pair)
    col = np.arange(pair)[None, :]
    same_chunk = (col // ch) == (t[:, None] // ch)

    def prefix(idx):
        return (same_chunk & (col <= idx[:, None])).astype(np.float32)

    last = (t // ch) * ch + ch - 1
    blocks = [prefix(t), prefix(last) - prefix(t)]
    levels = _hgrn_levels()
    for h in levels:
        if h == 1:
            continue
        ref = (t & -(2 * h)) + h - 1
        sign = np.where((t & h) != 0, 1.0, -1.0)[:, None]
        blocks.append(sign * (prefix(t) - prefix(ref)))
    op = np.concatenate(blocks, axis=0)
    op = np.concatenate([op, op], axis=1)

    ti = np.arange(ch)[:, None]
    sj = np.arange(pair)[None, :] & (ch - 1)
    masks = [(ti == sj)]
    for h in levels:
        masks.append((((ti ^ sj) & -(2 * h)) == 0) & ((ti & h) != 0) & ((sj & h) == 0))
    masks.append((((ti ^ sj) & -(ch // 2)) == 0) & (sj <= ti))
    return jnp.asarray(op, BF16), jnp.asarray(np.stack(masks), F32)


def _hgrn_kernel(op_ref, mask_ref, q_ref, k_ref, v_ref, gate_ref, lf_ref, gn_ref, o_ref, st_ref):
    seq = q_ref.shape[1]
    ch = HGRN_CHUNK
    pair = 2 * ch
    st_ref[...] = jnp.zeros_like(st_ref)
    gain = gn_ref[...]
    halves = _hgrn_levels()
    first_chunk = lax.broadcasted_iota(jnp.int32, (ch, pair), 1) < ch
    row_id = lax.broadcasted_iota(jnp.int32, (pair, LANES), 0)
    odd_row = (row_id & 1) == 1
    first_half = (row_id & (ch // 2)) == 0
    zeros_half = jnp.zeros((ch, LANES), BF16)

    def side_by_side(x):
        return jnp.concatenate([x[:ch], x[ch:]], axis=1)

    def block_diag(x):
        return jnp.concatenate([jnp.concatenate([x[:ch], zeros_half], axis=1),
                                jnp.concatenate([zeros_half, x[ch:]], axis=1)], axis=0)

    group = min(HGRN_GROUP, seq // pair)

    def pair_rows(gi, c):
        return pl.ds(pl.multiple_of(gi * (group * pair), pair) + c * pair, pair)

    def chunk_bcast(b, row):
        return jnp.concatenate([jnp.broadcast_to(b[c0 + row:c0 + row + 1, :], (ch, LANES))
                                for c0 in (0, ch)], axis=0)

    def scores_any_decay(qb, kb, cum, g):
        attn = _dot_nt(side_by_side(qb), block_diag(kb)) * mask_ref[0]
        for li, h in enumerate(halves):
            if h == 1:
                e = jnp.where(odd_row, jnp.exp2(g), 1.0).astype(BF16)
            else:
                e = jnp.exp2(cum[(2 + li) * pair:(3 + li) * pair]).astype(BF16)
            attn = attn + _dot_nt(side_by_side(qb * e), block_diag(kb * e)) * mask_ref[1 + li]
        b = cum[:pair]
        return attn, qb * jnp.exp2(b).astype(BF16), kb * jnp.exp2(cum[pair:2 * pair]).astype(BF16)

    def scores_bounded_decay(qb, kb, b):
        mid = ch // 2
        d = b - chunk_bcast(b, mid - 1)
        e = jnp.exp2(-jnp.abs(d)).astype(BF16)
        q_top = qb * e
        attn = _dot_nt(side_by_side(q_top), block_diag(kb * e)) * mask_ref[1]
        q_dec = qb * jnp.exp2(b).astype(BF16)
        q_loc = jnp.where(first_half, q_dec, q_top)
        k_loc = kb * jnp.exp2(jnp.where(first_half, -b, -d)).astype(BF16)
        attn = attn + _dot_nt(side_by_side(q_loc), block_diag(k_loc)) * mask_ref[len(halves) + 1]
        k_dec = kb * jnp.exp2(chunk_bcast(b, ch - 1) - b).astype(BF16)
        return attn, q_dec, k_dec

    def step(gi, carry, bounded):
        op = op_ref[:pair, :] if bounded else op_ref[...]
        logs, cums = [], []
        for c in range(group):
            g = lf_ref[0, pair_rows(gi, c), :]
            g_hi = g.astype(BF16)
            g_lo = (g - g_hi.astype(F32)).astype(BF16)
            logs.append(g)
            cums.append(_dot(op, jnp.concatenate([g_hi, g_lo], axis=0)))

        lhs, decays = [], []
        for c in range(group):
            rows = pair_rows(gi, c)
            qb = q_ref[0, rows, :]
            kb = k_ref[0, rows, :]
            b = cums[c][:pair]
            if bounded:
                attn, q_dec, k_dec = scores_bounded_decay(qb, kb, b)
            else:
                attn, q_dec, k_dec = scores_any_decay(qb, kb, cums[c], logs[c])
            cross = _dot_nt(q_dec[ch:], k_dec)
            lhs.append(jnp.concatenate([jnp.where(first_chunk, attn, 0.0),
                                        jnp.where(first_chunk, cross, attn)], axis=0).astype(BF16))
            tot0 = jnp.exp2(b[ch - 1:ch, :])
            tot1 = jnp.exp2(b[pair - 1:pair, :])
            q_in = jnp.concatenate([q_dec[:ch], q_dec[ch:] * tot0.astype(BF16)], axis=0)
            k_up = jnp.concatenate([k_dec[:ch] * tot1.astype(BF16), k_dec[ch:]], axis=0)
            decays.append((q_in, k_up, tot0 * tot1))

        st = st_ref[...]
        for c in range(group):
            rows = pair_rows(gi, c)
            v = v_ref[0, rows, :]
            q_in, k_up, total = decays[c]
            o = _dot_nt(q_in, st.astype(BF16)) + _dot(lhs[c], v)
            st = st * total + _dot_tn(v, k_up)
            y = _rms(o, gain) * gate_ref[0, rows, :].astype(F32)
            o_ref[0, rows, :] = y.astype(BF16)
        st_ref[...] = st
        return carry

    half_rows = ch // 2
    half_tot = jnp.sum(lf_ref[0].reshape(seq // half_rows, half_rows, LANES), axis=1)
    worst = jnp.min(jnp.min(half_tot, axis=0, keepdims=True), axis=1, keepdims=True)
    bounded_ok = worst[0, 0] >= HGRN_MIN_HALF_LOG2
    n_steps = seq // (group * pair)

    @pl.when(bounded_ok)
    def _():
        lax.fori_loop(0, n_steps, functools.partial(step, bounded=True), 0)

    @pl.when(jnp.logical_not(bounded_ok))
    def _():
        lax.fori_loop(0, n_steps, functools.partial(step, bounded=False), 0)


def _hgrn_recurrence(proj, logf, gnorm):
    b, seq, _ = proj.shape
    h = HGRN_HEADS
    spec = lambda off: pl.BlockSpec((1, seq, LANES), lambda bi, hi: (bi, 0, off + hi))
    op, masks = _hgrn_tables()
    return pl.pallas_call(
        _hgrn_kernel,
        grid=(b, h),
        in_specs=[_resident(op.shape), _resident(masks.shape),
                  spec(0), spec(h), spec(2 * h), spec(3 * h), spec(0), _resident(gnorm.shape)],
        out_specs=pl.BlockSpec((1, seq, LANES), lambda bi, hi: (bi, 0, hi)),
        out_shape=jax.ShapeDtypeStruct((b, seq, h * HGRN_VAL_DIM), BF16),
        scratch_shapes=[pltpu.VMEM((HGRN_VAL_DIM, HGRN_KEY_DIM), F32)],
        compiler_params=_params(("parallel", "parallel")),
        name="hgrn_recurrence",
    )(op, masks, proj, proj, proj, proj, logf, gnorm)


def _post_kernel(*refs, n_mix, hidden, chunks):
    mix_refs = refs[:n_mix]
    wo_ref, x_ref, g_ref, win_ref, wout_ref, o_ref = refs[n_mix:]
    gains = g_ref[...]
    mixed = jnp.concatenate([r[...] for r in mix_refs], axis=-1) if n_mix > 1 else mix_refs[0][...]
    x1 = x_ref[...] + _rms(_dot(mixed, wo_ref[...]), gains[1:2])
    h = _rms(x1, gains[2:3]).astype(BF16)
    acc = jnp.zeros(x1.shape, F32)
    for lo, hi in chunks:
        gate = _dot(h, win_ref[:, lo:hi])
        up = _dot(h, win_ref[:, hidden + lo:hidden + hi])
        act = (gate * _sigmoid(gate) * up).astype(BF16)
        acc = acc + _dot(act, wout_ref[lo:hi, :])
    o_ref[...] = x1 + _rms(acc, gains[3:4])


def _post_mixer(mix_list, w_out, mixer_layer, x2d, gains, ffn_w_in, ffn_w_out, layer, seq):
    t, d = x2d.shape
    hidden = ffn_w_out.shape[1]
    tm = min(ROW_TILE, seq)
    chunks = tuple((lo, min(lo + COL_CHUNK, hidden)) for lo in range(0, hidden, COL_CHUNK))
    mix_specs = [pl.BlockSpec((tm, m.shape[1]), lambda i: (i, 0)) for m in mix_list]
    return pl.pallas_call(
        functools.partial(_post_kernel, n_mix=len(mix_list), hidden=hidden, chunks=chunks),
        grid=(t // tm,),
        in_specs=mix_specs + [_layer_block(w_out, mixer_layer),
                              pl.BlockSpec((tm, d), lambda i: (i, 0)),
                              _layer_block(gains, layer),
                              _layer_block(ffn_w_in, layer),
                              _layer_block(ffn_w_out, layer)],
        out_specs=pl.BlockSpec((tm, d), lambda i: (i, 0)),
        out_shape=jax.ShapeDtypeStruct((t, d), F32),
        compiler_params=_params(("parallel",)),
        name="post_mixer_ffn",
    )(*mix_list, w_out, x2d, gains, ffn_w_in, ffn_w_out)


def _rope_tables(seq):
    half = HEAD_DIM // 2
    inv_freq = ROPE_THETA ** (-jnp.arange(0, HEAD_DIM, 2, dtype=F32) / HEAD_DIM)
    ang = jnp.arange(seq, dtype=F32)[:, None] * inv_freq[None, :]
    reps = LANES // half
    sign = jnp.tile(jnp.concatenate([-jnp.ones((half,), F32), jnp.ones((half,), F32)]), LANES // HEAD_DIM)
    return jnp.tile(jnp.cos(ang), (1, reps)), jnp.tile(jnp.sin(ang), (1, reps)) * sign[None, :]


def kernel(x, norm_gains, att_w_in, att_lambda, att_subln, att_w_out, rec_w_in, rec_lower_bounds,
           rec_gnorm, rec_w_out, ffn_w_in, ffn_w_out):
    b, seq, d = x.shape
    depth = norm_gains.shape[0]
    cos, sin = _rope_tables(seq)
    x2d = x.reshape(b * seq, d)
    att_w_in, att_w_out, rec_w_in, rec_w_out, ffn_w_in, ffn_w_out = (
        w.astype(BF16) for w in (att_w_in, att_w_out, rec_w_in, rec_w_out, ffn_w_in, ffn_w_out))
    for layer in range(depth):
        gain_in = norm_gains[layer, 0:1]
        j = layer // 2
        if layer % 2 == 0:
            lambda_init = 0.8 - 0.6 * math.exp(-0.3 * layer)
            proj = _att_in_proj(x2d, gain_in, att_w_in, j, cos, sin, seq)
            proj = proj.reshape(b, seq, -1)
            oa = _dilated_attention(proj)
            od = _diff_attention(proj, att_lambda[j], att_subln[j][None, :], lambda_init)
            mix = [oa.reshape(b * seq, -1), od.reshape(b * seq, -1)]
            w_out = att_w_out
        else:
            proj, logf = _rec_in_proj(x2d, gain_in, rec_w_in, rec_lower_bounds, j, seq)
            o = _hgrn_recurrence(proj.reshape(b, seq, -1), logf.reshape(b, seq, -1), rec_gnorm[j][None, :])
            mix = [o.reshape(b * seq, -1)]
            w_out = rec_w_out
        x2d = _post_mixer(mix, w_out, j, x2d, norm_gains, ffn_w_in, ffn_w_out, layer, seq)
    return x2d.reshape(b, seq, d)
```

```python
import functools
import math

import jax
import jax.numpy as jnp
import numpy as np
from jax import lax
from jax.experimental import pallas as pl
from jax.experimental.pallas import tpu as pltpu

F32 = jnp.float32
BF16 = jnp.bfloat16

HEAD_DIM = 64
A_HEADS = 8
A_WIDTH = A_HEADS * HEAD_DIM
DILATED_PATTERNS = ((128, 1), (512, 4), (2048, 16))
DIL_BLOCK = 128
DIFF_HEADS = 4
DIFF_QK_WIDTH = 2 * DIFF_HEADS * HEAD_DIM
DIFF_V_DIM = 2 * HEAD_DIM
DIFF_V_WIDTH = DIFF_HEADS * DIFF_V_DIM
ROPE_THETA = 10000.0
HGRN_HEADS = 8
HGRN_KEY_DIM = 128
HGRN_VAL_DIM = 128
NORM_EPS = 1e-6
NEG_INF = -1e30
LOG2_E = math.log2(math.e)

LANES = 128
BF16_ROWS = 16
V7X_VMEM_BYTES = 64 * 1024 * 1024
VMEM_LIMIT = V7X_VMEM_BYTES * 7 // 8

ROW_TILE = 512
COL_CHUNK = 512
DIL_GROUP = 8
DIFF_Q_TILE = 256
DIFF_LOOKAHEAD = 2
HGRN_CHUNK = 64
HGRN_GROUP = 16
HGRN_MIN_HALF_LOG2 = -96.0


def _rms(x, gain):
    ms = jnp.mean(x * x, axis=-1, keepdims=True)
    return x * lax.rsqrt(ms + NORM_EPS) * gain


def _sigmoid(x):
    return 1.0 / (1.0 + jnp.exp(-x))


def _dot(a, b):
    return jnp.dot(a, b, preferred_element_type=F32)


def _dot_nt(a, b):
    return lax.dot_general(a, b, (((1,), (1,)), ((), ())), preferred_element_type=F32)


def _dot_tn(a, b):
    return lax.dot_general(a, b, (((0,), (0,)), ((), ())), preferred_element_type=F32)


def _params(semantics):
    return pltpu.CompilerParams(dimension_semantics=semantics, vmem_limit_bytes=VMEM_LIMIT)


def _resident(shape):
    return pl.BlockSpec(shape, lambda *_: (0,) * len(shape), pipeline_mode=pl.Buffered(1))


def _layer_block(stack, layer):
    tail = stack.shape[1:]
    return pl.BlockSpec((None,) + tail, lambda *_: (layer,) + (0,) * len(tail), pipeline_mode=pl.Buffered(1))


def _att_in_kernel(x_ref, g_ref, w_ref, cos_ref, sin_ref, o_ref):
    h = _rms(x_ref[...], g_ref[...]).astype(BF16)
    tm = h.shape[0]
    cos = cos_ref[...]
    sin = sin_ref[...]
    lane = lax.broadcasted_iota(jnp.int32, (tm, LANES), 1)
    first_half = (lane % HEAD_DIM) < (HEAD_DIM // 2)
    n_cols = w_ref.shape[1]
    q_scale = HEAD_DIM ** -0.5 * LOG2_E
    rope_hi = 2 * A_WIDTH
    diff_lo = 3 * A_WIDTH
    diff_rope_hi = diff_lo + 2 * DIFF_QK_WIDTH
    for c0 in range(0, n_cols, COL_CHUNK):
        acc = _dot(h, w_ref[:, c0:c0 + COL_CHUNK])
        for l0 in range(0, COL_CHUNK, LANES):
            col = c0 + l0
            y = acc[:, l0:l0 + LANES]
            if col < rope_hi or diff_lo <= col < diff_rope_hi:
                rot = jnp.where(first_half, pltpu.roll(y, LANES - HEAD_DIM // 2, 1),
                                pltpu.roll(y, HEAD_DIM // 2, 1))
                y = y * cos + rot * sin
                if col < A_WIDTH or diff_lo <= col < diff_lo + DIFF_QK_WIDTH:
                    y = y * q_scale
            o_ref[:, col:col + LANES] = y.astype(BF16)


def _att_in_proj(x2d, gain, w_stack, layer, cos, sin, seq):
    t, d = x2d.shape
    n = w_stack.shape[2]
    tm = min(ROW_TILE, seq)
    tiles_per_seq = seq // tm
    return pl.pallas_call(
        _att_in_kernel,
        grid=(t // tm,),
        in_specs=[pl.BlockSpec((tm, d), lambda i: (i, 0)),
                  _resident((1, d)),
                  _layer_block(w_stack, layer),
                  pl.BlockSpec((tm, LANES), lambda i: (i % tiles_per_seq, 0)),
                  pl.BlockSpec((tm, LANES), lambda i: (i % tiles_per_seq, 0))],
        out_specs=pl.BlockSpec((tm, n), lambda i: (i, 0)),
        out_shape=jax.ShapeDtypeStruct((t, n), BF16),
        compiler_params=_params(("parallel",)),
        name="att_in_proj",
    )(x2d, gain, w_stack, cos, sin)


def _dilated_kernel(q_ref, k_ref, v_ref, o_ref, qf, kf, vf, acc, mx, den):
    seq = q_ref.shape[1]
    blk = DIL_BLOCK
    qf[...] = q_ref[0].astype(F32)
    kf[...] = k_ref[0].astype(F32)
    vf[...] = v_ref[0].astype(F32)

    lane = lax.broadcasted_iota(jnp.int32, (blk, LANES), 1)
    head_a = lane < HEAD_DIM
    qi = lax.broadcasted_iota(jnp.int32, (2 * blk, 2 * blk), 0) % blk
    ci = lax.broadcasted_iota(jnp.int32, (2 * blk, 2 * blk), 1)
    mask_two = (ci >= qi) & (ci <= qi + blk)
    qi1 = lax.broadcasted_iota(jnp.int32, (2 * blk, blk), 0) % blk
    ci1 = lax.broadcasted_iota(jnp.int32, (2 * blk, blk), 1)
    mask_one = ci1 <= qi1

    zero = jnp.zeros((blk, LANES), BF16)

    def attend(p, blocks):
        scores = []
        for qb, kc, _, _, _ in blocks:
            q2 = jnp.concatenate([jnp.where(head_a, qb, zero), jnp.where(head_a, zero, qb)], axis=0)
            scores.append(_dot_nt(q2, kc))
        probs = []
        for s, (_, _, _, first, _) in zip(scores, blocks):
            s = jnp.where(mask_one if first else mask_two, s, NEG_INF)
            m = jnp.max(s, axis=-1, keepdims=True)
            probs.append((jnp.exp2(s - m).astype(BF16), m))
        for (e, m), (_, _, vc, _, rows) in zip(probs, blocks):
            o = _dot(e, jnp.concatenate([vc, jnp.ones_like(vc)], axis=1))
            acc[p, rows, :] = jnp.where(head_a, o[:blk, :LANES], o[blk:, :LANES])
            mx[p, rows, :] = jnp.where(head_a, m[:blk], m[blk:])
            den[p, rows, :] = jnp.where(head_a, o[:blk, LANES:], o[blk:, LANES:])

    for p, (window, dil) in enumerate(DILATED_PATTERNS):
        assert window // dil == blk
        length = seq // dil
        nb = length // blk

        if dil == 1:
            for n0 in range(0, nb, DIL_GROUP):
                blocks = []
                for n in range(n0, min(n0 + DIL_GROUP, nb)):
                    keys = slice(max(n - 1, 0) * blk, (n + 1) * blk)
                    blocks.append((q_ref[0, n * blk:(n + 1) * blk, :], k_ref[0, keys, :], v_ref[0, keys, :],
                                   n == 0, pl.ds(n * blk, blk)))
                attend(p, blocks)
            continue

        per_step = max(DIL_GROUP // nb, 1)

        def residues(step, carry, p=p, dil=dil, length=length, nb=nb, per_step=per_step):
            blocks = []
            for j in range(per_step):
                r = step * per_step + j
                qr = qf[pl.ds(r, length, stride=dil), :].astype(BF16)
                kr = kf[pl.ds(r, length, stride=dil), :].astype(BF16)
                vr = vf[pl.ds(r, length, stride=dil), :].astype(BF16)
                for n in range(nb):
                    keys = slice(max(n - 1, 0) * blk, (n + 1) * blk)
                    blocks.append((qr[n * blk:(n + 1) * blk], kr[keys], vr[keys], n == 0,
                                   pl.ds(r + n * blk * dil, blk, stride=dil)))
            attend(p, blocks)
            return carry

        lax.fori_loop(0, dil // per_step, residues, 0)

    m_all = jnp.maximum(jnp.maximum(mx[0], mx[1]), mx[2])
    num = jnp.zeros((seq, LANES), F32)
    tot = jnp.zeros((seq, LANES), F32)
    for p in range(len(DILATED_PATTERNS)):
        w = jnp.exp2(mx[p] - m_all)
        num = num + w * acc[p]
        tot = tot + w * den[p]
    o_ref[0] = (num / tot).astype(BF16)


def _dilated_attention(proj):
    b, seq, _ = proj.shape
    n_pairs = A_WIDTH // LANES
    spec = lambda off: pl.BlockSpec((1, seq, LANES), lambda bi, pi: (bi, 0, off + pi))
    n_pat = len(DILATED_PATTERNS)
    return pl.pallas_call(
        _dilated_kernel,
        grid=(b, n_pairs),
        in_specs=[spec(0), spec(n_pairs), spec(2 * n_pairs)],
        out_specs=pl.BlockSpec((1, seq, LANES), lambda bi, pi: (bi, 0, pi)),
        out_shape=jax.ShapeDtypeStruct((b, seq, A_WIDTH), BF16),
        scratch_shapes=[pltpu.VMEM((seq, LANES), F32)] * 3
        + [pltpu.VMEM((n_pat, seq, LANES), F32)] * 3,
        compiler_params=_params(("parallel", "parallel")),
        name="dilated_attention",
    )(proj, proj, proj)


def _diff_kernel(lam_ref, q_ref, k_ref, v_ref, sg_ref, o_ref, vt_ref, *, lambda_init):
    lp = lam_ref[...]
    l1 = jnp.sum(lp[0:1] * lp[1:2], axis=-1, keepdims=True)
    l2 = jnp.sum(lp[2:3] * lp[3:4], axis=-1, keepdims=True)
    lam = jnp.exp(l1) - jnp.exp(l2) + lambda_init
    seq = q_ref.shape[1]
    tq = min(DIFF_Q_TILE, seq)
    lane = lax.broadcasted_iota(jnp.int32, (tq, LANES), 1)
    sub0 = lane < HEAD_DIM
    zero = jnp.zeros((tq, LANES), BF16)
    gain = sg_ref[...] * (1.0 - lambda_init)

    vt_ref[0:LANES, :] = v_ref[0].astype(F32).T.astype(BF16)
    pad_rows = vt_ref.shape[0] - LANES
    vt_ref[LANES:, :] = jnp.where(lax.broadcasted_iota(jnp.int32, (pad_rows, seq), 0) == 0, 1.0, 0.0).astype(BF16)

    causal = (lax.broadcasted_iota(jnp.int32, (tq, 2 * tq), 0)
              <= lax.broadcasted_iota(jnp.int32, (tq, 2 * tq), 1) % tq)

    def scores(i):
        q = q_ref[0, i * tq:(i + 1) * tq, :]
        q2 = jnp.concatenate([jnp.where(sub0, q, zero), jnp.where(sub0, zero, q)], axis=0)
        past = i * tq
        s_diag = jnp.where(causal, _dot_nt(k_ref[0, past:past + tq, :], q2), NEG_INF)
        s_past = _dot_nt(k_ref[0, 0:past, :], q2) if past else None
        return s_diag, s_past

    n_tiles = seq // tq
    ahead = [scores(i) for i in range(min(DIFF_LOOKAHEAD, n_tiles))]
    for i in range(n_tiles):
        s_diag, s_past = ahead.pop(0)
        if i + DIFF_LOOKAHEAD < n_tiles:
            ahead.append(scores(i + DIFF_LOOKAHEAD))
        past = i * tq
        m = jnp.max(s_diag, axis=0, keepdims=True)
        if past:
            m = jnp.maximum(m, jnp.max(s_past, axis=0, keepdims=True))
        o = _dot(vt_ref[:, past:past + tq], jnp.exp2(s_diag - m).astype(BF16))
        if past:
            o = o + _dot(vt_ref[:, 0:past], jnp.exp2(s_past - m).astype(BF16))
        o = o[:LANES] / o[LANES:LANES + 1]
        w = o[:, :tq] - lam * o[:, tq:]
        ms = jnp.mean(w * w, axis=0, keepdims=True)
        y = w * lax.rsqrt(ms + NORM_EPS) * gain
        o_ref[0, i * tq:(i + 1) * tq, :] = y.T.astype(BF16)


def _diff_attention(proj, lam_params, subln, lambda_init):
    b, seq, _ = proj.shape
    q_off = 3 * A_WIDTH // LANES
    k_off = q_off + DIFF_QK_WIDTH // LANES
    v_off = k_off + DIFF_QK_WIDTH // LANES
    spec = lambda off: pl.BlockSpec((1, seq, LANES), lambda bi, hi: (bi, 0, off + hi))
    return pl.pallas_call(
        functools.partial(_diff_kernel, lambda_init=lambda_init),
        grid=(b, DIFF_HEADS),
        in_specs=[_resident(lam_params.shape), spec(q_off), spec(k_off), spec(v_off),
                  _resident(subln.shape)],
        out_specs=pl.BlockSpec((1, seq, LANES), lambda bi, hi: (bi, 0, hi)),
        out_shape=jax.ShapeDtypeStruct((b, seq, DIFF_V_WIDTH), BF16),
        scratch_shapes=[pltpu.VMEM((DIFF_V_DIM + BF16_ROWS, seq), BF16)],
        compiler_params=_params(("parallel", "parallel")),
        name="diff_attention",
    )(lam_params, proj, proj, proj, subln)


def _rec_in_kernel(x_ref, g_ref, w_ref, lb_ref, o_ref, lf_ref, *, layer):
    h = _rms(x_ref[...], g_ref[...]).astype(BF16)
    lb_raw = lb_ref[...]
    lb_e = jnp.exp(lb_raw - jnp.max(lb_raw, axis=0, keepdims=True))
    lb_p = lb_e / jnp.sum(lb_e, axis=0, keepdims=True)
    lower = jnp.sum(lb_p[0:layer + 1], axis=0, keepdims=True) - lb_p[0:1]
    width = lf_ref.shape[1]
    n_cols = w_ref.shape[1]
    q_scale = HGRN_KEY_DIM ** -0.5
    starts = sorted(range(0, n_cols, COL_CHUNK), key=lambda c: (c // width == 2, c))
    for c0 in starts:
        y = _dot(h, w_ref[:, c0:c0 + COL_CHUNK])
        group = c0 // width
        if group == 0:
            o_ref[:, c0:c0 + COL_CHUNK] = (y * _sigmoid(y) * q_scale).astype(BF16)
        elif group == 1:
            f0 = c0 - width
            lo = lower[:, f0:f0 + COL_CHUNK]
            forget = lo + (1.0 - lo) * _sigmoid(y)
            lf_ref[:, f0:f0 + COL_CHUNK] = jnp.log2(forget)
            o_ref[:, c0:c0 + COL_CHUNK] = (1.0 - forget).astype(BF16)
        elif group == 2:
            o_ref[:, c0:c0 + COL_CHUNK] = y.astype(BF16)
        else:
            o_ref[:, c0:c0 + COL_CHUNK] = (y * _sigmoid(y)).astype(BF16)


def _rec_in_proj(x2d, gain, w_stack, lower_bounds, layer, seq):
    t, d = x2d.shape
    n = w_stack.shape[2]
    width = lower_bounds.shape[1]
    tm = min(ROW_TILE, seq)
    return pl.pallas_call(
        functools.partial(_rec_in_kernel, layer=layer),
        grid=(t // tm,),
        in_specs=[pl.BlockSpec((tm, d), lambda i: (i, 0)),
                  _resident((1, d)),
                  _layer_block(w_stack, layer),
                  _resident(lower_bounds.shape)],
        out_specs=[pl.BlockSpec((tm, n), lambda i: (i, 0)),
                   pl.BlockSpec((tm, width), lambda i: (i, 0))],
        out_shape=[jax.ShapeDtypeStruct((t, n), BF16),
                   jax.ShapeDtypeStruct((t, width), F32)],
        compiler_params=_params(("parallel",)),
        name="rec_in_proj",
    )(x2d, gain, w_stack, lower_bounds)


def _hgrn_levels():
    return [HGRN_CHUNK >> (i + 1) for i in range(HGRN_CHUNK.bit_length() - 1)]


def _hgrn_tables():
    ch = HGRN_CHUNK
    pair = 2 * ch
    t = np.arange(pair)
    col = np.arange(pair)[None, :]
    same_chunk = (col // ch) == (t[:, None] // ch)

    def prefix(idx):
        return (same_chunk & (col <= idx[:, None])).astype(np.float32)

    last = (t // ch) * ch + ch - 1
    blocks = [prefix(t), prefix(last) - prefix(t)]
    levels = _hgrn_levels()
    for h in levels:
        if h == 1:
            continue
        ref = (t & -(2 * h)) + h - 1
        sign = np.where((t & h) != 0, 1.0, -1.0)[:, None]
        blocks.append(sign * (prefix(t) - prefix(ref)))
    op = np.concatenate(blocks, axis=0)
    op = np.concatenate([op, op], axis=1)

    ti = np.arange(ch)[:, None]
    sj = np.arange(pair)[None, :] & (ch - 1)
    masks = [(ti == sj)]
    for h in levels:
        masks.append((((ti ^ sj) & -(2 * h)) == 0) & ((ti & h) != 0) & ((sj & h) == 0))
    masks.append((((ti ^ sj) & -(ch // 2)) == 0) & (sj <= ti))
    return jnp.asarray(op, BF16), jnp.asarray(np.stack(masks), F32)


def _hgrn_kernel(op_ref, mask_ref, q_ref, k_ref, v_ref, gate_ref, lf_ref, gn_ref, o_ref, st_ref):
    seq = q_ref.shape[1]
    ch = HGRN_CHUNK
    pair = 2 * ch
    st_ref[...] = jnp.zeros_like(st_ref)
    gain = gn_ref[...]
    halves = _hgrn_levels()
    first_chunk = lax.broadcasted_iota(jnp.int32, (ch, pair), 1) < ch
    row_id = lax.broadcasted_iota(jnp.int32, (pair, LANES), 0)
    odd_row = (row_id & 1) == 1
    first_half = (row_id & (ch // 2)) == 0
    zeros_half = jnp.zeros((ch, LANES), BF16)

    def side_by_side(x):
        return jnp.concatenate([x[:ch], x[ch:]], axis=1)

    def block_diag(x):
        return jnp.concatenate([jnp.concatenate([x[:ch], zeros_half], axis=1),
                                jnp.concatenate([zeros_half, x[ch:]], axis=1)], axis=0)

    group = min(HGRN_GROUP, seq // pair)

    def pair_rows(gi, c):
        return pl.ds(pl.multiple_of(gi * (group * pair), pair) + c * pair, pair)

    def chunk_bcast(b, row):
        return jnp.concatenate([jnp.broadcast_to(b[c0 + row:c0 + row + 1, :], (ch, LANES))
                                for c0 in (0, ch)], axis=0)

    def scores_any_decay(qb, kb, cum, g):
        attn = _dot_nt(side_by_side(qb), block_diag(kb)) * mask_ref[0]
        for li, h in enumerate(halves):
            if h == 1:
                e = jnp.where(odd_row, jnp.exp2(g), 1.0).astype(BF16)
            else:
                e = jnp.exp2(cum[(2 + li) * pair:(3 + li) * pair]).astype(BF16)
            attn = attn + _dot_nt(side_by_side(qb * e), block_diag(kb * e)) * mask_ref[1 + li]
        b = cum[:pair]
        return attn, qb * jnp.exp2(b).astype(BF16), kb * jnp.exp2(cum[pair:2 * pair]).astype(BF16)

    def scores_bounded_decay(qb, kb, b):
        mid = ch // 2
        d = b - chunk_bcast(b, mid - 1)
        e = jnp.exp2(-jnp.abs(d)).astype(BF16)
        q_top = qb * e
        attn = _dot_nt(side_by_side(q_top), block_diag(kb * e)) * mask_ref[1]
        q_dec = qb * jnp.exp2(b).astype(BF16)
        q_loc = jnp.where(first_half, q_dec, q_top)
        k_loc = kb * jnp.exp2(jnp.where(first_half, -b, -d)).astype(BF16)
        attn = attn + _dot_nt(side_by_side(q_loc), block_diag(k_loc)) * mask_ref[len(halves) + 1]
        k_dec = kb * jnp.exp2(chunk_bcast(b, ch - 1) - b).astype(BF16)
        return attn, q_dec, k_dec

    def step(gi, carry, bounded):
        op = op_ref[:pair, :] if bounded else op_ref[...]
        logs, cums = [], []
        for c in range(group):
            g = lf_ref[0, pair_rows(gi, c), :]
            g_hi = g.astype(BF16)
            g_lo = (g - g_hi.astype(F32)).astype(BF16)
            logs.append(g)
            cums.append(_dot(op, jnp.concatenate([g_hi, g_lo], axis=0)))

        lhs, decays = [], []
        for c in range(group):
            rows = pair_rows(gi, c)
            qb = q_ref[0, rows, :]
            kb = k_ref[0, rows, :]
            b = cums[c][:pair]
            if bounded:
                attn, q_dec, k_dec = scores_bounded_decay(qb, kb, b)
            else:
                attn, q_dec, k_dec = scores_any_decay(qb, kb, cums[c], logs[c])
            cross = _dot_nt(q_dec[ch:], k_dec)
            lhs.append(jnp.concatenate([jnp.where(first_chunk, attn, 0.0),
                                        jnp.where(first_chunk, cross, attn)], axis=0).astype(BF16))
            tot0 = jnp.exp2(b[ch - 1:ch, :])
            tot1 = jnp.exp2(b[pair - 1:pair, :])
            q_in = jnp.concatenate([q_dec[:ch], q_dec[ch:] * tot0.astype(BF16)], axis=0)
            k_up = jnp.concatenate([k_dec[:ch] * tot1.astype(BF16), k_dec[ch:]], axis=0)
            decays.append((q_in, k_up, tot0 * tot1))

        st = st_ref[...]
        for c in range(group):
            rows = pair_rows(gi, c)
            v = v_ref[0, rows, :]
            q_in, k_up, total = decays[c]
            o = _dot_nt(q_in, st.astype(BF16)) + _dot(lhs[c], v)
            st = st * total + _dot_tn(v, k_up)
            y = _rms(o, gain) * gate_ref[0, rows, :].astype(F32)
            o_ref[0, rows, :] = y.astype(BF16)
        st_ref[...] = st
        return carry

    half_rows = ch // 2
    half_tot = jnp.sum(lf_ref[0].reshape(seq // half_rows, half_rows, LANES), axis=1)
    worst = jnp.min(jnp.min(half_tot, axis=0, keepdims=True), axis=1, keepdims=True)
    bounded_ok = worst[0, 0] >= HGRN_MIN_HALF_LOG2
    n_steps = seq // (group * pair)

    @pl.when(bounded_ok)
    def _():
        lax.fori_loop(0, n_steps, functools.partial(step, bounded=True), 0)

    @pl.when(jnp.logical_not(bounded_ok))
    def _():
        lax.fori_loop(0, n_steps, functools.partial(step, bounded=False), 0)


def _hgrn_recurrence(proj, logf, gnorm):
    b, seq, _ = proj.shape
    h = HGRN_HEADS
    spec = lambda off: pl.BlockSpec((1, seq, LANES), lambda bi, hi: (bi, 0, off + hi))
    op, masks = _hgrn_tables()
    return pl.pallas_call(
        _hgrn_kernel,
        grid=(b, h),
        in_specs=[_resident(op.shape), _resident(masks.shape),
                  spec(0), spec(h), spec(2 * h), spec(3 * h), spec(0), _resident(gnorm.shape)],
        out_specs=pl.BlockSpec((1, seq, LANES), lambda bi, hi: (bi, 0, hi)),
        out_shape=jax.ShapeDtypeStruct((b, seq, h * HGRN_VAL_DIM), BF16),
        scratch_shapes=[pltpu.VMEM((HGRN_VAL_DIM, HGRN_KEY_DIM), F32)],
        compiler_params=_params(("parallel", "parallel")),
        name="hgrn_recurrence",
    )(op, masks, proj, proj, proj, proj, logf, gnorm)


def _post_kernel(*refs, n_mix, hidden, chunks):
    mix_refs = refs[:n_mix]
    wo_ref, x_ref, g_ref, win_ref, wout_ref, o_ref = refs[n_mix:]
    gains = g_ref[...]
    mixed = jnp.concatenate([r[...] for r in mix_refs], axis=-1) if n_mix > 1 else mix_refs[0][...]
    x1 = x_ref[...] + _rms(_dot(mixed, wo_ref[...]), gains[1:2])
    h = _rms(x1, gains[2:3]).astype(BF16)
    acc = jnp.zeros(x1.shape, F32)
    for lo, hi in chunks:
        gate = _dot(h, win_ref[:, lo:hi])
        up = _dot(h, win_ref[:, hidden + lo:hidden + hi])
        act = (gate * _sigmoid(gate) * up).astype(BF16)
        acc = acc + _dot(act, wout_ref[lo:hi, :])
    o_ref[...] = x1 + _rms(acc, gains[3:4])


def _post_mixer(mix_list, w_out, mixer_layer, x2d, gains, ffn_w_in, ffn_w_out, layer, seq):
    t, d = x2d.shape
    hidden = ffn_w_out.shape[1]
    tm = min(ROW_TILE, seq)
    chunks = tuple((lo, min(lo + COL_CHUNK, hidden)) for lo in range(0, hidden, COL_CHUNK))
    mix_specs = [pl.BlockSpec((tm, m.shape[1]), lambda i: (i, 0)) for m in mix_list]
    return pl.pallas_call(
        functools.partial(_post_kernel, n_mix=len(mix_list), hidden=hidden, chunks=chunks),
        grid=(t // tm,),
        in_specs=mix_specs + [_layer_block(w_out, mixer_layer),
                              pl.BlockSpec((tm, d), lambda i: (i, 0)),
                              _layer_block(gains, layer),
                              _layer_block(ffn_w_in, layer),
                              _layer_block(ffn_w_out, layer)],
        out_specs=pl.BlockSpec((tm, d), lambda i: (i, 0)),
        out_shape=jax.ShapeDtypeStruct((t, d), F32),
        compiler_params=_params(("parallel",)),
        name="post_mixer_ffn",
    )(*mix_list, w_out, x2d, gains, ffn_w_in, ffn_w_out)


def _rope_tables(seq):
    half = HEAD_DIM // 2
    inv_freq = ROPE_THETA ** (-jnp.arange(0, HEAD_DIM, 2, dtype=F32) / HEAD_DIM)
    ang = jnp.arange(seq, dtype=F32)[:, None] * inv_freq[None, :]
    reps = LANES // half
    sign = jnp.tile(jnp.concatenate([-jnp.ones((half,), F32), jnp.ones((half,), F32)]), LANES // HEAD_DIM)
    return jnp.tile(jnp.cos(ang), (1, reps)), jnp.tile(jnp.sin(ang), (1, reps)) * sign[None, :]


def kernel(x, norm_gains, att_w_in, att_lambda, att_subln, att_w_out, rec_w_in, rec_lower_bounds,
           rec_gnorm, rec_w_out, ffn_w_in, ffn_w_out):
    b, seq, d = x.shape
    depth = norm_gains.shape[0]
    cos, sin = _rope_tables(seq)
    x2d = x.reshape(b * seq, d)
    att_w_in, att_w_out, rec_w_in, rec_w_out, ffn_w_in, ffn_w_out = (
        w.astype(BF16) for w in (att_w_in, att_w_out, rec_w_in, rec_w_out, ffn_w_in, ffn_w_out))
    for layer in range(depth):
        gain_in = norm_gains[layer, 0:1]
        j = layer // 2
        if layer % 2 == 0:
            lambda_init = 0.8 - 0.6 * math.exp(-0.3 * layer)
            proj = _att_in_proj(x2d, gain_in, att_w_in, j, cos, sin, seq)
            proj = proj.reshape(b, seq, -1)
            oa = _dilated_attention(proj)
            od = _diff_attention(proj, att_lambda[j], att_subln[j][:, None], lambda_init)
            mix = [oa.reshape(b * seq, -1), od.reshape(b * seq, -1)]
            w_out = att_w_out
        else:
            proj, logf = _rec_in_proj(x2d, gain_in, rec_w_in, rec_lower_bounds, j, seq)
            o = _hgrn_recurrence(proj.reshape(b, seq, -1), logf.reshape(b, seq, -1), rec_gnorm[j][None, :])
            mix = [o.reshape(b * seq, -1)]
            w_out = rec_w_out
        x2d = _post_mixer(mix, w_out, j, x2d, norm_gains, ffn_w_in, ffn_w_out, layer, seq)
    return x2d.reshape(b, seq, d)
```

```python
import functools
import math

import jax
import jax.numpy as jnp
import numpy as np
from jax import lax
from jax.experimental import pallas as pl
from jax.experimental.pallas import tpu as pltpu

F32 = jnp.float32
BF16 = jnp.bfloat16

HEAD_DIM = 64
A_HEADS = 8
A_WIDTH = A_HEADS * HEAD_DIM
DILATED_PATTERNS = ((128, 1), (512, 4), (2048, 16))
DIL_BLOCK = 128
DIFF_HEADS = 4
DIFF_QK_WIDTH = 2 * DIFF_HEADS * HEAD_DIM
DIFF_V_DIM = 2 * HEAD_DIM
DIFF_V_WIDTH = DIFF_HEADS * DIFF_V_DIM
ROPE_THETA = 10000.0
HGRN_HEADS = 8
HGRN_KEY_DIM = 128
HGRN_VAL_DIM = 128
NORM_EPS = 1e-6
NEG_INF = -1e30
LOG2_E = math.log2(math.e)

LANES = 128
BF16_ROWS = 16
V7X_VMEM_BYTES = 64 * 1024 * 1024
VMEM_LIMIT = V7X_VMEM_BYTES * 7 // 8

ROW_TILE = 512
COL_CHUNK = 512
DIL_GROUP = 8
DIFF_Q_TILE = 256
DIFF_LOOKAHEAD = 2
HGRN_CHUNK = 64
HGRN_GROUP = 16
HGRN_AHEAD = (2, 2)
HGRN_MIN_HALF_LOG2 = -96.0


def _rms(x, gain):
    ms = jnp.mean(x * x, axis=-1, keepdims=True)
    return x * lax.rsqrt(ms + NORM_EPS) * gain


def _sigmoid(x):
    return 1.0 / (1.0 + jnp.exp(-x))


def _dot(a, b):
    return jnp.dot(a, b, preferred_element_type=F32)


def _dot_nt(a, b):
    return lax.dot_general(a, b, (((1,), (1,)), ((), ())), preferred_element_type=F32)


def _dot_tn(a, b):
    return lax.dot_general(a, b, (((0,), (0,)), ((), ())), preferred_element_type=F32)


def _params(semantics):
    return pltpu.CompilerParams(dimension_semantics=semantics, vmem_limit_bytes=VMEM_LIMIT)


def _resident(shape):
    return pl.BlockSpec(shape, lambda *_: (0,) * len(shape), pipeline_mode=pl.Buffered(1))


def _layer_block(stack, layer):
    tail = stack.shape[1:]
    return pl.BlockSpec((None,) + tail, lambda *_: (layer,) + (0,) * len(tail), pipeline_mode=pl.Buffered(1))


def _att_in_kernel(x_ref, g_ref, w_ref, cos_ref, sin_ref, o_ref):
    h = _rms(x_ref[...], g_ref[...]).astype(BF16)
    tm = h.shape[0]
    cos = cos_ref[...]
    sin = sin_ref[...]
    lane = lax.broadcasted_iota(jnp.int32, (tm, LANES), 1)
    first_half = (lane % HEAD_DIM) < (HEAD_DIM // 2)
    n_cols = w_ref.shape[1]
    q_scale = HEAD_DIM ** -0.5 * LOG2_E
    rope_hi = 2 * A_WIDTH
    diff_lo = 3 * A_WIDTH
    diff_rope_hi = diff_lo + 2 * DIFF_QK_WIDTH
    for c0 in range(0, n_cols, COL_CHUNK):
        acc = _dot(h, w_ref[:, c0:c0 + COL_CHUNK])
        for l0 in range(0, COL_CHUNK, LANES):
            col = c0 + l0
            y = acc[:, l0:l0 + LANES]
            if col < rope_hi or diff_lo <= col < diff_rope_hi:
                rot = jnp.where(first_half, pltpu.roll(y, LANES - HEAD_DIM // 2, 1),
                                pltpu.roll(y, HEAD_DIM // 2, 1))
                y = y * cos + rot * sin
                if col < A_WIDTH or diff_lo <= col < diff_lo + DIFF_QK_WIDTH:
                    y = y * q_scale
            o_ref[:, col:col + LANES] = y.astype(BF16)


def _att_in_proj(x2d, gain, w_stack, layer, cos, sin, seq):
    t, d = x2d.shape
    n = w_stack.shape[2]
    tm = min(ROW_TILE, seq)
    tiles_per_seq = seq // tm
    return pl.pallas_call(
        _att_in_kernel,
        grid=(t // tm,),
        in_specs=[pl.BlockSpec((tm, d), lambda i: (i, 0)),
                  _resident((1, d)),
                  _layer_block(w_stack, layer),
                  pl.BlockSpec((tm, LANES), lambda i: (i % tiles_per_seq, 0)),
                  pl.BlockSpec((tm, LANES), lambda i: (i % tiles_per_seq, 0))],
        out_specs=pl.BlockSpec((tm, n), lambda i: (i, 0)),
        out_shape=jax.ShapeDtypeStruct((t, n), BF16),
        compiler_params=_params(("parallel",)),
        name="att_in_proj",
    )(x2d, gain, w_stack, cos, sin)


def _dilated_kernel(q_ref, k_ref, v_ref, o_ref, qf, kf, vf, acc, mx, den):
    seq = q_ref.shape[1]
    blk = DIL_BLOCK
    qf[...] = q_ref[0].astype(F32)
    kf[...] = k_ref[0].astype(F32)
    vf[...] = v_ref[0].astype(F32)

    lane = lax.broadcasted_iota(jnp.int32, (blk, LANES), 1)
    head_a = lane < HEAD_DIM
    qi = lax.broadcasted_iota(jnp.int32, (2 * blk, 2 * blk), 0) % blk
    ci = lax.broadcasted_iota(jnp.int32, (2 * blk, 2 * blk), 1)
    mask_two = (ci >= qi) & (ci <= qi + blk)
    qi1 = lax.broadcasted_iota(jnp.int32, (2 * blk, blk), 0) % blk
    ci1 = lax.broadcasted_iota(jnp.int32, (2 * blk, blk), 1)
    mask_one = ci1 <= qi1

    zero = jnp.zeros((blk, LANES), BF16)

    def attend(p, blocks):
        scores = []
        for qb, kc, _, _, _ in blocks:
            q2 = jnp.concatenate([jnp.where(head_a, qb, zero), jnp.where(head_a, zero, qb)], axis=0)
            scores.append(_dot_nt(q2, kc))
        probs = []
        for s, (_, _, _, first, _) in zip(scores, blocks):
            s = jnp.where(mask_one if first else mask_two, s, NEG_INF)
            m = jnp.max(s, axis=-1, keepdims=True)
            probs.append((jnp.exp2(s - m).astype(BF16), m))
        for (e, m), (_, _, vc, _, rows) in zip(probs, blocks):
            o = _dot(e, jnp.concatenate([vc, jnp.ones_like(vc)], axis=1))
            acc[p, rows, :] = jnp.where(head_a, o[:blk, :LANES], o[blk:, :LANES])
            mx[p, rows, :] = jnp.where(head_a, m[:blk], m[blk:])
            den[p, rows, :] = jnp.where(head_a, o[:blk, LANES:], o[blk:, LANES:])

    for p, (window, dil) in enumerate(DILATED_PATTERNS):
        assert window // dil == blk
        length = seq // dil
        nb = length // blk

        if dil == 1:
            for n0 in range(0, nb, DIL_GROUP):
                blocks = []
                for n in range(n0, min(n0 + DIL_GROUP, nb)):
                    keys = slice(max(n - 1, 0) * blk, (n + 1) * blk)
                    blocks.append((q_ref[0, n * blk:(n + 1) * blk, :], k_ref[0, keys, :], v_ref[0, keys, :],
                                   n == 0, pl.ds(n * blk, blk)))
                attend(p, blocks)
            continue

        per_step = max(DIL_GROUP // nb, 1)

        def residues(step, carry, p=p, dil=dil, length=length, nb=nb, per_step=per_step):
            blocks = []
            for j in range(per_step):
                r = step * per_step + j
                qr = qf[pl.ds(r, length, stride=dil), :].astype(BF16)
                kr = kf[pl.ds(r, length, stride=dil), :].astype(BF16)
                vr = vf[pl.ds(r, length, stride=dil), :].astype(BF16)
                for n in range(nb):
                    keys = slice(max(n - 1, 0) * blk, (n + 1) * blk)
                    blocks.append((qr[n * blk:(n + 1) * blk], kr[keys], vr[keys], n == 0,
                                   pl.ds(r + n * blk * dil, blk, stride=dil)))
            attend(p, blocks)
            return carry

        lax.fori_loop(0, dil // per_step, residues, 0)

    m_all = jnp.maximum(jnp.maximum(mx[0], mx[1]), mx[2])
    num = jnp.zeros((seq, LANES), F32)
    tot = jnp.zeros((seq, LANES), F32)
    for p in range(len(DILATED_PATTERNS)):
        w = jnp.exp2(mx[p] - m_all)
        num = num + w * acc[p]
        tot = tot + w * den[p]
    o_ref[0] = (num / tot).astype(BF16)


def _dilated_attention(proj):
    b, seq, _ = proj.shape
    n_pairs = A_WIDTH // LANES
    spec = lambda off: pl.BlockSpec((1, seq, LANES), lambda bi, pi: (bi, 0, off + pi))
    n_pat = len(DILATED_PATTERNS)
    return pl.pallas_call(
        _dilated_kernel,
        grid=(b, n_pairs),
        in_specs=[spec(0), spec(n_pairs), spec(2 * n_pairs)],
        out_specs=pl.BlockSpec((1, seq, LANES), lambda bi, pi: (bi, 0, pi)),
        out_shape=jax.ShapeDtypeStruct((b, seq, A_WIDTH), BF16),
        scratch_shapes=[pltpu.VMEM((seq, LANES), F32)] * 3
        + [pltpu.VMEM((n_pat, seq, LANES), F32)] * 3,
        compiler_params=_params(("parallel", "parallel")),
        name="dilated_attention",
    )(proj, proj, proj)


def _diff_kernel(lam_ref, q_ref, k_ref, v_ref, sg_ref, o_ref, vt_ref, *, lambda_init):
    lp = lam_ref[...]
    l1 = jnp.sum(lp[0:1] * lp[1:2], axis=-1, keepdims=True)
    l2 = jnp.sum(lp[2:3] * lp[3:4], axis=-1, keepdims=True)
    lam = jnp.exp(l1) - jnp.exp(l2) + lambda_init
    seq = q_ref.shape[1]
    tq = min(DIFF_Q_TILE, seq)
    lane = lax.broadcasted_iota(jnp.int32, (tq, LANES), 1)
    sub0 = lane < HEAD_DIM
    zero = jnp.zeros((tq, LANES), BF16)
    gain = sg_ref[...] * (1.0 - lambda_init)

    vt_ref[0:LANES, :] = v_ref[0].astype(F32).T.astype(BF16)
    pad_rows = vt_ref.shape[0] - LANES
    vt_ref[LANES:, :] = jnp.where(lax.broadcasted_iota(jnp.int32, (pad_rows, seq), 0) == 0, 1.0, 0.0).astype(BF16)

    causal = (lax.broadcasted_iota(jnp.int32, (tq, 2 * tq), 0)
              <= lax.broadcasted_iota(jnp.int32, (tq, 2 * tq), 1) % tq)

    def scores(i):
        q = q_ref[0, i * tq:(i + 1) * tq, :]
        q2 = jnp.concatenate([jnp.where(sub0, q, zero), jnp.where(sub0, zero, q)], axis=0)
        past = i * tq
        s_diag = jnp.where(causal, _dot_nt(k_ref[0, past:past + tq, :], q2), NEG_INF)
        s_past = _dot_nt(k_ref[0, 0:past, :], q2) if past else None
        return s_diag, s_past

    n_tiles = seq // tq
    ahead = [scores(i) for i in range(min(DIFF_LOOKAHEAD, n_tiles))]
    for i in range(n_tiles):
        s_diag, s_past = ahead.pop(0)
        if i + DIFF_LOOKAHEAD < n_tiles:
            ahead.append(scores(i + DIFF_LOOKAHEAD))
        past = i * tq
        m = jnp.max(s_diag, axis=0, keepdims=True)
        if past:
            m = jnp.maximum(m, jnp.max(s_past, axis=0, keepdims=True))
        o = _dot(vt_ref[:, past:past + tq], jnp.exp2(s_diag - m).astype(BF16))
        if past:
            o = o + _dot(vt_ref[:, 0:past], jnp.exp2(s_past - m).astype(BF16))
        o = o[:LANES] / o[LANES:LANES + 1]
        w = o[:, :tq] - lam * o[:, tq:]
        ms = jnp.mean(w * w, axis=0, keepdims=True)
        y = w * lax.rsqrt(ms + NORM_EPS) * gain
        o_ref[0, i * tq:(i + 1) * tq, :] = y.T.astype(BF16)


def _diff_attention(proj, lam_params, subln, lambda_init):
    b, seq, _ = proj.shape
    q_off = 3 * A_WIDTH // LANES
    k_off = q_off + DIFF_QK_WIDTH // LANES
    v_off = k_off + DIFF_QK_WIDTH // LANES
    spec = lambda off: pl.BlockSpec((1, seq, LANES), lambda bi, hi: (bi, 0, off + hi))
    return pl.pallas_call(
        functools.partial(_diff_kernel, lambda_init=lambda_init),
        grid=(b, DIFF_HEADS),
        in_specs=[_resident(lam_params.shape), spec(q_off), spec(k_off), spec(v_off),
                  _resident(subln.shape)],
        out_specs=pl.BlockSpec((1, seq, LANES), lambda bi, hi: (bi, 0, hi)),
        out_shape=jax.ShapeDtypeStruct((b, seq, DIFF_V_WIDTH), BF16),
        scratch_shapes=[pltpu.VMEM((DIFF_V_DIM + BF16_ROWS, seq), BF16)],
        compiler_params=_params(("parallel", "parallel")),
        name="diff_attention",
    )(lam_params, proj, proj, proj, subln)


def _rec_in_kernel(x_ref, g_ref, w_ref, lb_ref, o_ref, lf_ref, *, layer):
    h = _rms(x_ref[...], g_ref[...]).astype(BF16)
    lb_raw = lb_ref[...]
    lb_e = jnp.exp(lb_raw - jnp.max(lb_raw, axis=0, keepdims=True))
    lb_p = lb_e / jnp.sum(lb_e, axis=0, keepdims=True)
    lower = jnp.sum(lb_p[0:layer + 1], axis=0, keepdims=True) - lb_p[0:1]
    width = lf_ref.shape[1]
    n_cols = w_ref.shape[1]
    q_scale = HGRN_KEY_DIM ** -0.5
    starts = sorted(range(0, n_cols, COL_CHUNK), key=lambda c: (c // width == 2, c))
    for c0 in starts:
        y = _dot(h, w_ref[:, c0:c0 + COL_CHUNK])
        group = c0 // width
        if group == 0:
            o_ref[:, c0:c0 + COL_CHUNK] = (y * _sigmoid(y) * q_scale).astype(BF16)
        elif group == 1:
            f0 = c0 - width
            lo = lower[:, f0:f0 + COL_CHUNK]
            forget = lo + (1.0 - lo) * _sigmoid(y)
            lf_ref[:, f0:f0 + COL_CHUNK] = jnp.log2(forget)
            o_ref[:, c0:c0 + COL_CHUNK] = (1.0 - forget).astype(BF16)
        elif group == 2:
            o_ref[:, c0:c0 + COL_CHUNK] = y.astype(BF16)
        else:
            o_ref[:, c0:c0 + COL_CHUNK] = (y * _sigmoid(y)).astype(BF16)


def _rec_in_proj(x2d, gain, w_stack, lower_bounds, layer, seq):
    t, d = x2d.shape
    n = w_stack.shape[2]
    width = lower_bounds.shape[1]
    tm = min(ROW_TILE, seq)
    return pl.pallas_call(
        functools.partial(_rec_in_kernel, layer=layer),
        grid=(t // tm,),
        in_specs=[pl.BlockSpec((tm, d), lambda i: (i, 0)),
                  _resident((1, d)),
                  _layer_block(w_stack, layer),
                  _resident(lower_bounds.shape)],
        out_specs=[pl.BlockSpec((tm, n), lambda i: (i, 0)),
                   pl.BlockSpec((tm, width), lambda i: (i, 0))],
        out_shape=[jax.ShapeDtypeStruct((t, n), BF16),
                   jax.ShapeDtypeStruct((t, width), F32)],
        compiler_params=_params(("parallel",)),
        name="rec_in_proj",
    )(x2d, gain, w_stack, lower_bounds)


def _hgrn_levels():
    return [HGRN_CHUNK >> (i + 1) for i in range(HGRN_CHUNK.bit_length() - 1)]


def _hgrn_tables():
    ch = HGRN_CHUNK
    pair = 2 * ch
    t = np.arange(pair)
    col = np.arange(pair)[None, :]
    same_chunk = (col // ch) == (t[:, None] // ch)

    def prefix(idx):
        return (same_chunk & (col <= idx[:, None])).astype(np.float32)

    last = (t // ch) * ch + ch - 1
    blocks = [prefix(t), prefix(last) - prefix(t)]
    levels = _hgrn_levels()
    for h in levels:
        if h == 1:
            continue
        ref = (t & -(2 * h)) + h - 1
        sign = np.where((t & h) != 0, 1.0, -1.0)[:, None]
        blocks.append(sign * (prefix(t) - prefix(ref)))
    op = np.concatenate(blocks, axis=0)
    op = np.concatenate([op, op], axis=1)

    ti = np.arange(ch)[:, None]
    sj = np.arange(pair)[None, :] & (ch - 1)
    masks = [(ti == sj)]
    for h in levels:
        masks.append((((ti ^ sj) & -(2 * h)) == 0) & ((ti & h) != 0) & ((sj & h) == 0))
    masks.append((((ti ^ sj) & -(ch // 2)) == 0) & (sj <= ti))
    return jnp.asarray(op, BF16), jnp.asarray(np.stack(masks), F32)


def _hgrn_kernel(op_ref, mask_ref, q_ref, k_ref, v_ref, gate_ref, lf_ref, gn_ref, o_ref, st_ref):
    seq = q_ref.shape[1]
    ch = HGRN_CHUNK
    pair = 2 * ch
    st_ref[...] = jnp.zeros_like(st_ref)
    gain = gn_ref[...]
    halves = _hgrn_levels()
    first_chunk = lax.broadcasted_iota(jnp.int32, (ch, pair), 1) < ch
    row_id = lax.broadcasted_iota(jnp.int32, (pair, LANES), 0)
    odd_row = (row_id & 1) == 1
    first_half = (row_id & (ch // 2)) == 0
    zeros_half = jnp.zeros((ch, LANES), BF16)

    def side_by_side(x):
        return jnp.concatenate([x[:ch], x[ch:]], axis=1)

    def block_diag(x):
        return jnp.concatenate([jnp.concatenate([x[:ch], zeros_half], axis=1),
                                jnp.concatenate([zeros_half, x[ch:]], axis=1)], axis=0)

    group = min(HGRN_GROUP, seq // pair)

    def pair_rows(gi, c):
        return pl.ds(pl.multiple_of(gi * (group * pair), pair) + c * pair, pair)

    def chunk_bcast(b, row):
        return jnp.concatenate([jnp.broadcast_to(b[c0 + row:c0 + row + 1, :], (ch, LANES))
                                for c0 in (0, ch)], axis=0)

    def scores_any_decay(qb, kb, cum, g):
        attn = _dot_nt(side_by_side(qb), block_diag(kb)) * mask_ref[0]
        for li, h in enumerate(halves):
            if h == 1:
                e = jnp.where(odd_row, jnp.exp2(g), 1.0).astype(BF16)
            else:
                e = jnp.exp2(cum[(2 + li) * pair:(3 + li) * pair]).astype(BF16)
            attn = attn + _dot_nt(side_by_side(qb * e), block_diag(kb * e)) * mask_ref[1 + li]
        b = cum[:pair]
        return attn, qb * jnp.exp2(b).astype(BF16), kb * jnp.exp2(cum[pair:2 * pair]).astype(BF16)

    def scores_bounded_decay(qb, kb, b):
        mid = ch // 2
        d = b - chunk_bcast(b, mid - 1)
        e = jnp.exp2(-jnp.abs(d)).astype(BF16)
        q_top = qb * e
        attn = _dot_nt(side_by_side(q_top), block_diag(kb * e)) * mask_ref[1]
        q_dec = qb * jnp.exp2(b).astype(BF16)
        q_loc = jnp.where(first_half, q_dec, q_top)
        k_loc = kb * jnp.exp2(jnp.where(first_half, -b, -d)).astype(BF16)
        attn = attn + _dot_nt(side_by_side(q_loc), block_diag(k_loc)) * mask_ref[len(halves) + 1]
        k_dec = kb * jnp.exp2(chunk_bcast(b, ch - 1) - b).astype(BF16)
        return attn, q_dec, k_dec

    def decay_exponents(rows, bounded):
        op = op_ref[:pair, :] if bounded else op_ref[...]
        g = lf_ref[0, rows, :]
        g_hi = g.astype(BF16)
        g_lo = (g - g_hi.astype(F32)).astype(BF16)
        return g, _dot(op, jnp.concatenate([g_hi, g_lo], axis=0))

    def pair_scores(rows, g, cum, bounded):
        qb = q_ref[0, rows, :]
        kb = k_ref[0, rows, :]
        b = cum[:pair]
        if bounded:
            attn, q_dec, k_dec = scores_bounded_decay(qb, kb, b)
        else:
            attn, q_dec, k_dec = scores_any_decay(qb, kb, cum, g)
        cross = _dot_nt(q_dec[ch:], k_dec)
        lhs = jnp.concatenate([jnp.where(first_chunk, attn, 0.0),
                               jnp.where(first_chunk, cross, attn)], axis=0).astype(BF16)
        tot0 = jnp.exp2(b[ch - 1:ch, :])
        tot1 = jnp.exp2(b[pair - 1:pair, :])
        q_in = jnp.concatenate([q_dec[:ch], q_dec[ch:] * tot0.astype(BF16)], axis=0)
        k_up = jnp.concatenate([k_dec[:ch] * tot1.astype(BF16), k_dec[ch:]], axis=0)
        return lhs, q_in, k_up, tot0 * tot1

    def step(gi, carry, bounded):
        ahead_exp, ahead_sc = HGRN_AHEAD
        exps, scs = {}, {}
        st = st_ref[...]
        for tick in range(group + ahead_exp + ahead_sc):
            if tick < group:
                exps[tick] = decay_exponents(pair_rows(gi, tick), bounded)
            c = tick - ahead_exp
            if 0 <= c < group:
                scs[c] = pair_scores(pair_rows(gi, c), *exps.pop(c), bounded)
            c = tick - ahead_exp - ahead_sc
            if 0 <= c < group:
                rows = pair_rows(gi, c)
                v = v_ref[0, rows, :]
                lhs, q_in, k_up, total = scs.pop(c)
                o = _dot_nt(q_in, st.astype(BF16)) + _dot(lhs, v)
                st = st * total + _dot_tn(v, k_up)
                y = _rms(o, gain) * gate_ref[0, rows, :].astype(F32)
                o_ref[0, rows, :] = y.astype(BF16)
        st_ref[...] = st
        return carry

    half_rows = ch // 2
    half_tot = jnp.sum(lf_ref[0].reshape(seq // half_rows, half_rows, LANES), axis=1)
    worst = jnp.min(jnp.min(half_tot, axis=0, keepdims=True), axis=1, keepdims=True)
    bounded_ok = worst[0, 0] >= HGRN_MIN_HALF_LOG2
    n_steps = seq // (group * pair)

    @pl.when(bounded_ok)
    def _():
        lax.fori_loop(0, n_steps, functools.partial(step, bounded=True), 0)

    @pl.when(jnp.logical_not(bounded_ok))
    def _():
        lax.fori_loop(0, n_steps, functools.partial(step, bounded=False), 0)


def _hgrn_recurrence(proj, logf, gnorm):
    b, seq, _ = proj.shape
    h = HGRN_HEADS
    spec = lambda off: pl.BlockSpec((1, seq, LANES), lambda bi, hi: (bi, 0, off + hi))
    op, masks = _hgrn_tables()
    return pl.pallas_call(
        _hgrn_kernel,
        grid=(b, h),
        in_specs=[_resident(op.shape), _resident(masks.shape),
                  spec(0), spec(h), spec(2 * h), spec(3 * h), spec(0), _resident(gnorm.shape)],
        out_specs=pl.BlockSpec((1, seq, LANES), lambda bi, hi: (bi, 0, hi)),
        out_shape=jax.ShapeDtypeStruct((b, seq, h * HGRN_VAL_DIM), BF16),
        scratch_shapes=[pltpu.VMEM((HGRN_VAL_DIM, HGRN_KEY_DIM), F32)],
        compiler_params=_params(("parallel", "parallel")),
        name="hgrn_recurrence",
    )(op, masks, proj, proj, proj, proj, logf, gnorm)


def _post_kernel(*refs, n_mix, hidden, chunks):
    mix_refs = refs[:n_mix]
    wo_ref, x_ref, g_ref, win_ref, wout_ref, o_ref = refs[n_mix:]
    gains = g_ref[...]
    mixed = jnp.concatenate([r[...] for r in mix_refs], axis=-1) if n_mix > 1 else mix_refs[0][...]
    x1 = x_ref[...] + _rms(_dot(mixed, wo_ref[...]), gains[1:2])
    h = _rms(x1, gains[2:3]).astype(BF16)
    acc = jnp.zeros(x1.shape, F32)
    for lo, hi in chunks:
        gate = _dot(h, win_ref[:, lo:hi])
        up = _dot(h, win_ref[:, hidden + lo:hidden + hi])
        act = (gate * _sigmoid(gate) * up).astype(BF16)
        acc = acc + _dot(act, wout_ref[lo:hi, :])
    o_ref[...] = x1 + _rms(acc, gains[3:4])


def _post_mixer(mix_list, w_out, mixer_layer, x2d, gains, ffn_w_in, ffn_w_out, layer, seq):
    t, d = x2d.shape
    hidden = ffn_w_out.shape[1]
    tm = min(ROW_TILE, seq)
    chunks = tuple((lo, min(lo + COL_CHUNK, hidden)) for lo in range(0, hidden, COL_CHUNK))
    mix_specs = [pl.BlockSpec((tm, m.shape[1]), lambda i: (i, 0)) for m in mix_list]
    return pl.pallas_call(
        functools.partial(_post_kernel, n_mix=len(mix_list), hidden=hidden, chunks=chunks),
        grid=(t // tm,),
        in_specs=mix_specs + [_layer_block(w_out, mixer_layer),
                              pl.BlockSpec((tm, d), lambda i: (i, 0)),
                              _layer_block(gains, layer),
                              _layer_block(ffn_w_in, layer),
                              _layer_block(ffn_w_out, layer)],
        out_specs=pl.BlockSpec((tm, d), lambda i: (i, 0)),
        out_shape=jax.ShapeDtypeStruct((t, d), F32),
        compiler_params=_params(("parallel",)),
        name="post_mixer_ffn",
    )(*mix_list, w_out, x2d, gains, ffn_w_in, ffn_w_out)


def _rope_tables(seq):
    half = HEAD_DIM // 2
    inv_freq = ROPE_THETA ** (-jnp.arange(0, HEAD_DIM, 2, dtype=F32) / HEAD_DIM)
    ang = jnp.arange(seq, dtype=F32)[:, None] * inv_freq[None, :]
    reps = LANES // half
    sign = jnp.tile(jnp.concatenate([-jnp.ones((half,), F32), jnp.ones((half,), F32)]), LANES // HEAD_DIM)
    return jnp.tile(jnp.cos(ang), (1, reps)), jnp.tile(jnp.sin(ang), (1, reps)) * sign[None, :]


def kernel(x, norm_gains, att_w_in, att_lambda, att_subln, att_w_out, rec_w_in, rec_lower_bounds,
           rec_gnorm, rec_w_out, ffn_w_in, ffn_w_out):
    b, seq, d = x.shape
    depth = norm_gains.shape[0]
    cos, sin = _rope_tables(seq)
    x2d = x.reshape(b * seq, d)
    att_w_in, att_w_out, rec_w_in, rec_w_out, ffn_w_in, ffn_w_out = (
        w.astype(BF16) for w in (att_w_in, att_w_out, rec_w_in, rec_w_out, ffn_w_in, ffn_w_out))
    for layer in range(depth):
        gain_in = norm_gains[layer, 0:1]
        j = layer // 2
        if layer % 2 == 0:
            lambda_init = 0.8 - 0.6 * math.exp(-0.3 * layer)
            proj = _att_in_proj(x2d, gain_in, att_w_in, j, cos, sin, seq)
            proj = proj.reshape(b, seq, -1)
            oa = _dilated_attention(proj)
            od = _diff_attention(proj, att_lambda[j], att_subln[j][:, None], lambda_init)
            mix = [oa.reshape(b * seq, -1), od.reshape(b * seq, -1)]
            w_out = att_w_out
        else:
            proj, logf = _rec_in_proj(x2d, gain_in, rec_w_in, rec_lower_bounds, j, seq)
            o = _hgrn_recurrence(proj.reshape(b, seq, -1), logf.reshape(b, seq, -1), rec_gnorm[j][None, :])
            mix = [o.reshape(b * seq, -1)]
            w_out = rec_w_out
        x2d = _post_mixer(mix, w_out, j, x2d, norm_gains, ffn_w_in, ffn_w_out, layer, seq)
    return x2d.reshape(b, seq, d)
```

```python
import functools
import math

import jax
import jax.numpy as jnp
import numpy as np
from jax import lax
from jax.experimental import pallas as pl
from jax.experimental.pallas import tpu as pltpu

F32 = jnp.float32
BF16 = jnp.bfloat16

HEAD_DIM = 64
A_HEADS = 8
A_WIDTH = A_HEADS * HEAD_DIM
DILATED_PATTERNS = ((128, 1), (512, 4), (2048, 16))
DIL_BLOCK = 128
DIFF_HEADS = 4
DIFF_QK_WIDTH = 2 * DIFF_HEADS * HEAD_DIM
DIFF_V_DIM = 2 * HEAD_DIM
DIFF_V_WIDTH = DIFF_HEADS * DIFF_V_DIM
ROPE_THETA = 10000.0
HGRN_HEADS = 8
HGRN_KEY_DIM = 128
HGRN_VAL_DIM = 128
NORM_EPS = 1e-6
NEG_INF = -1e30
LOG2_E = math.log2(math.e)

LANES = 128
BF16_ROWS = 16
V7X_VMEM_BYTES = 64 * 1024 * 1024
VMEM_LIMIT = V7X_VMEM_BYTES * 7 // 8

ROW_TILE = 512
COL_CHUNK = 512
DIL_GROUP = 8
DIFF_Q_TILE = 256
DIFF_LOOKAHEAD = 2
HGRN_CHUNK = 64
HGRN_GROUP = 16
HGRN_AHEAD = (2, 2)
HGRN_MIN_HALF_LOG2 = -96.0


def _rms(x, gain):
    ms = jnp.mean(x * x, axis=-1, keepdims=True)
    return x * lax.rsqrt(ms + NORM_EPS) * gain


def _sigmoid(x):
    return 1.0 / (1.0 + jnp.exp(-x))


def _dot(a, b):
    return jnp.dot(a, b, preferred_element_type=F32)


def _dot_nt(a, b):
    return lax.dot_general(a, b, (((1,), (1,)), ((), ())), preferred_element_type=F32)


def _dot_tn(a, b):
    return lax.dot_general(a, b, (((0,), (0,)), ((), ())), preferred_element_type=F32)


def _params(semantics):
    return pltpu.CompilerParams(dimension_semantics=semantics, vmem_limit_bytes=VMEM_LIMIT)


def _resident(shape):
    return pl.BlockSpec(shape, lambda *_: (0,) * len(shape), pipeline_mode=pl.Buffered(1))


def _layer_block(stack, layer):
    tail = stack.shape[1:]
    return pl.BlockSpec((None,) + tail, lambda *_: (layer,) + (0,) * len(tail), pipeline_mode=pl.Buffered(1))


def _att_in_kernel(x_ref, g_ref, w_ref, cos_ref, sin_ref, o_ref):
    h = _rms(x_ref[...], g_ref[...]).astype(BF16)
    tm = h.shape[0]
    cos = cos_ref[...]
    sin = sin_ref[...]
    lane = lax.broadcasted_iota(jnp.int32, (tm, LANES), 1)
    first_half = (lane % HEAD_DIM) < (HEAD_DIM // 2)
    n_cols = w_ref.shape[1]
    q_scale = HEAD_DIM ** -0.5 * LOG2_E
    rope_hi = 2 * A_WIDTH
    diff_lo = 3 * A_WIDTH
    diff_rope_hi = diff_lo + 2 * DIFF_QK_WIDTH
    for c0 in range(0, n_cols, COL_CHUNK):
        acc = _dot(h, w_ref[:, c0:c0 + COL_CHUNK].astype(BF16))
        for l0 in range(0, COL_CHUNK, LANES):
            col = c0 + l0
            y = acc[:, l0:l0 + LANES]
            if col < rope_hi or diff_lo <= col < diff_rope_hi:
                rot = jnp.where(first_half, pltpu.roll(y, LANES - HEAD_DIM // 2, 1),
                                pltpu.roll(y, HEAD_DIM // 2, 1))
                y = y * cos + rot * sin
                if col < A_WIDTH or diff_lo <= col < diff_lo + DIFF_QK_WIDTH:
                    y = y * q_scale
            o_ref[:, col:col + LANES] = y.astype(BF16)


def _att_in_proj(x2d, gain, w_stack, layer, cos, sin, seq):
    t, d = x2d.shape
    n = w_stack.shape[2]
    tm = min(ROW_TILE, seq)
    tiles_per_seq = seq // tm
    return pl.pallas_call(
        _att_in_kernel,
        grid=(t // tm,),
        in_specs=[pl.BlockSpec((tm, d), lambda i: (i, 0)),
                  _resident((1, d)),
                  _layer_block(w_stack, layer),
                  pl.BlockSpec((tm, LANES), lambda i: (i % tiles_per_seq, 0)),
                  pl.BlockSpec((tm, LANES), lambda i: (i % tiles_per_seq, 0))],
        out_specs=pl.BlockSpec((tm, n), lambda i: (i, 0)),
        out_shape=jax.ShapeDtypeStruct((t, n), BF16),
        compiler_params=_params(("parallel",)),
        name="att_in_proj",
    )(x2d, gain, w_stack, cos, sin)


def _dilated_kernel(q_ref, k_ref, v_ref, o_ref, qf, kf, vf, acc, mx, den):
    seq = q_ref.shape[1]
    blk = DIL_BLOCK
    qf[...] = q_ref[0].astype(F32)
    kf[...] = k_ref[0].astype(F32)
    vf[...] = v_ref[0].astype(F32)

    lane = lax.broadcasted_iota(jnp.int32, (blk, LANES), 1)
    head_a = lane < HEAD_DIM
    qi = lax.broadcasted_iota(jnp.int32, (2 * blk, 2 * blk), 0) % blk
    ci = lax.broadcasted_iota(jnp.int32, (2 * blk, 2 * blk), 1)
    mask_two = (ci >= qi) & (ci <= qi + blk)
    qi1 = lax.broadcasted_iota(jnp.int32, (2 * blk, blk), 0) % blk
    ci1 = lax.broadcasted_iota(jnp.int32, (2 * blk, blk), 1)
    mask_one = ci1 <= qi1

    zero = jnp.zeros((blk, LANES), BF16)

    def attend(p, blocks):
        scores = []
        for qb, kc, _, _, _ in blocks:
            q2 = jnp.concatenate([jnp.where(head_a, qb, zero), jnp.where(head_a, zero, qb)], axis=0)
            scores.append(_dot_nt(q2, kc))
        probs = []
        for s, (_, _, _, first, _) in zip(scores, blocks):
            s = jnp.where(mask_one if first else mask_two, s, NEG_INF)
            m = jnp.max(s, axis=-1, keepdims=True)
            probs.append((jnp.exp2(s - m).astype(BF16), m))
        for (e, m), (_, _, vc, _, rows) in zip(probs, blocks):
            o = _dot(e, jnp.concatenate([vc, jnp.ones_like(vc)], axis=1))
            acc[p, rows, :] = jnp.where(head_a, o[:blk, :LANES], o[blk:, :LANES])
            mx[p, rows, :] = jnp.where(head_a, m[:blk], m[blk:])
            den[p, rows, :] = jnp.where(head_a, o[:blk, LANES:], o[blk:, LANES:])

    for p, (window, dil) in enumerate(DILATED_PATTERNS):
        assert window // dil == blk
        length = seq // dil
        nb = length // blk

        if dil == 1:
            for n0 in range(0, nb, DIL_GROUP):
                blocks = []
                for n in range(n0, min(n0 + DIL_GROUP, nb)):
                    keys = slice(max(n - 1, 0) * blk, (n + 1) * blk)
                    blocks.append((q_ref[0, n * blk:(n + 1) * blk, :], k_ref[0, keys, :], v_ref[0, keys, :],
                                   n == 0, pl.ds(n * blk, blk)))
                attend(p, blocks)
            continue

        per_step = max(DIL_GROUP // nb, 1)

        def residues(step, carry, p=p, dil=dil, length=length, nb=nb, per_step=per_step):
            blocks = []
            for j in range(per_step):
                r = step * per_step + j
                qr = qf[pl.ds(r, length, stride=dil), :].astype(BF16)
                kr = kf[pl.ds(r, length, stride=dil), :].astype(BF16)
                vr = vf[pl.ds(r, length, stride=dil), :].astype(BF16)
                for n in range(nb):
                    keys = slice(max(n - 1, 0) * blk, (n + 1) * blk)
                    blocks.append((qr[n * blk:(n + 1) * blk], kr[keys], vr[keys], n == 0,
                                   pl.ds(r + n * blk * dil, blk, stride=dil)))
            attend(p, blocks)
            return carry

        lax.fori_loop(0, dil // per_step, residues, 0)

    m_all = jnp.maximum(jnp.maximum(mx[0], mx[1]), mx[2])
    num = jnp.zeros((seq, LANES), F32)
    tot = jnp.zeros((seq, LANES), F32)
    for p in range(len(DILATED_PATTERNS)):
        w = jnp.exp2(mx[p] - m_all)
        num = num + w * acc[p]
        tot = tot + w * den[p]
    o_ref[0] = (num / tot).astype(BF16)


def _dilated_attention(proj):
    b, seq, _ = proj.shape
    n_pairs = A_WIDTH // LANES
    spec = lambda off: pl.BlockSpec((1, seq, LANES), lambda bi, pi: (bi, 0, off + pi))
    n_pat = len(DILATED_PATTERNS)
    return pl.pallas_call(
        _dilated_kernel,
        grid=(b, n_pairs),
        in_specs=[spec(0), spec(n_pairs), spec(2 * n_pairs)],
        out_specs=pl.BlockSpec((1, seq, LANES), lambda bi, pi: (bi, 0, pi)),
        out_shape=jax.ShapeDtypeStruct((b, seq, A_WIDTH), BF16),
        scratch_shapes=[pltpu.VMEM((seq, LANES), F32)] * 3
        + [pltpu.VMEM((n_pat, seq, LANES), F32)] * 3,
        compiler_params=_params(("parallel", "parallel")),
        name="dilated_attention",
    )(proj, proj, proj)


def _diff_kernel(lam_ref, q_ref, k_ref, v_ref, sg_ref, o_ref, vt_ref, *, lambda_init):
    lp = lam_ref[...]
    l1 = jnp.sum(lp[0:1] * lp[1:2], axis=-1, keepdims=True)
    l2 = jnp.sum(lp[2:3] * lp[3:4], axis=-1, keepdims=True)
    lam = jnp.exp(l1) - jnp.exp(l2) + lambda_init
    seq = q_ref.shape[1]
    tq = min(DIFF_Q_TILE, seq)
    lane = lax.broadcasted_iota(jnp.int32, (tq, LANES), 1)
    sub0 = lane < HEAD_DIM
    zero = jnp.zeros((tq, LANES), BF16)
    gain = sg_ref[...] * (1.0 - lambda_init)

    vt_ref[0:LANES, :] = v_ref[0].astype(F32).T.astype(BF16)
    pad_rows = vt_ref.shape[0] - LANES
    vt_ref[LANES:, :] = jnp.where(lax.broadcasted_iota(jnp.int32, (pad_rows, seq), 0) == 0, 1.0, 0.0).astype(BF16)

    causal = (lax.broadcasted_iota(jnp.int32, (tq, 2 * tq), 0)
              <= lax.broadcasted_iota(jnp.int32, (tq, 2 * tq), 1) % tq)

    def scores(i):
        q = q_ref[0, i * tq:(i + 1) * tq, :]
        q2 = jnp.concatenate([jnp.where(sub0, q, zero), jnp.where(sub0, zero, q)], axis=0)
        past = i * tq
        s_diag = jnp.where(causal, _dot_nt(k_ref[0, past:past + tq, :], q2), NEG_INF)
        s_past = _dot_nt(k_ref[0, 0:past, :], q2) if past else None
        return s_diag, s_past

    n_tiles = seq // tq
    ahead = [scores(i) for i in range(min(DIFF_LOOKAHEAD, n_tiles))]
    for i in range(n_tiles):
        s_diag, s_past = ahead.pop(0)
        if i + DIFF_LOOKAHEAD < n_tiles:
            ahead.append(scores(i + DIFF_LOOKAHEAD))
        past = i * tq
        m = jnp.max(s_diag, axis=0, keepdims=True)
        if past:
            m = jnp.maximum(m, jnp.max(s_past, axis=0, keepdims=True))
        o = _dot(vt_ref[:, past:past + tq], jnp.exp2(s_diag - m).astype(BF16))
        if past:
            o = o + _dot(vt_ref[:, 0:past], jnp.exp2(s_past - m).astype(BF16))
        o = o[:LANES] / o[LANES:LANES + 1]
        w = o[:, :tq] - lam * o[:, tq:]
        ms = jnp.mean(w * w, axis=0, keepdims=True)
        y = w * lax.rsqrt(ms + NORM_EPS) * gain
        o_ref[0, i * tq:(i + 1) * tq, :] = y.T.astype(BF16)


def _diff_attention(proj, lam_params, subln, lambda_init):
    b, seq, _ = proj.shape
    q_off = 3 * A_WIDTH // LANES
    k_off = q_off + DIFF_QK_WIDTH // LANES
    v_off = k_off + DIFF_QK_WIDTH // LANES
    spec = lambda off: pl.BlockSpec((1, seq, LANES), lambda bi, hi: (bi, 0, off + hi))
    return pl.pallas_call(
        functools.partial(_diff_kernel, lambda_init=lambda_init),
        grid=(b, DIFF_HEADS),
        in_specs=[_resident(lam_params.shape), spec(q_off), spec(k_off), spec(v_off),
                  _resident(subln.shape)],
        out_specs=pl.BlockSpec((1, seq, LANES), lambda bi, hi: (bi, 0, hi)),
        out_shape=jax.ShapeDtypeStruct((b, seq, DIFF_V_WIDTH), BF16),
        scratch_shapes=[pltpu.VMEM((DIFF_V_DIM + BF16_ROWS, seq), BF16)],
        compiler_params=_params(("parallel", "parallel")),
        name="diff_attention",
    )(lam_params, proj, proj, proj, subln)


def _rec_in_kernel(x_ref, g_ref, w_ref, lb_ref, o_ref, lf_ref, *, layer):
    h = _rms(x_ref[...], g_ref[...]).astype(BF16)
    lb_raw = lb_ref[...]
    lb_e = jnp.exp(lb_raw - jnp.max(lb_raw, axis=0, keepdims=True))
    lb_p = lb_e / jnp.sum(lb_e, axis=0, keepdims=True)
    lower = jnp.sum(lb_p[0:layer + 1], axis=0, keepdims=True) - lb_p[0:1]
    width = lf_ref.shape[1]
    n_cols = w_ref.shape[1]
    q_scale = HGRN_KEY_DIM ** -0.5
    starts = sorted(range(0, n_cols, COL_CHUNK), key=lambda c: (c // width == 2, c))
    for c0 in starts:
        y = _dot(h, w_ref[:, c0:c0 + COL_CHUNK].astype(BF16))
        group = c0 // width
        if group == 0:
            o_ref[:, c0:c0 + COL_CHUNK] = (y * _sigmoid(y) * q_scale).astype(BF16)
        elif group == 1:
            f0 = c0 - width
            lo = lower[:, f0:f0 + COL_CHUNK]
            forget = lo + (1.0 - lo) * _sigmoid(y)
            lf_ref[:, f0:f0 + COL_CHUNK] = jnp.log2(forget)
            o_ref[:, c0:c0 + COL_CHUNK] = (1.0 - forget).astype(BF16)
        elif group == 2:
            o_ref[:, c0:c0 + COL_CHUNK] = y.astype(BF16)
        else:
            o_ref[:, c0:c0 + COL_CHUNK] = (y * _sigmoid(y)).astype(BF16)


def _rec_in_proj(x2d, gain, w_stack, lower_bounds, layer, seq):
    t, d = x2d.shape
    n = w_stack.shape[2]
    width = lower_bounds.shape[1]
    tm = min(ROW_TILE, seq)
    return pl.pallas_call(
        functools.partial(_rec_in_kernel, layer=layer),
        grid=(t // tm,),
        in_specs=[pl.BlockSpec((tm, d), lambda i: (i, 0)),
                  _resident((1, d)),
                  _layer_block(w_stack, layer),
                  _resident(lower_bounds.shape)],
        out_specs=[pl.BlockSpec((tm, n), lambda i: (i, 0)),
                   pl.BlockSpec((tm, width), lambda i: (i, 0))],
        out_shape=[jax.ShapeDtypeStruct((t, n), BF16),
                   jax.ShapeDtypeStruct((t, width), F32)],
        compiler_params=_params(("parallel",)),
        name="rec_in_proj",
    )(x2d, gain, w_stack, lower_bounds)


def _hgrn_levels():
    return [HGRN_CHUNK >> (i + 1) for i in range(HGRN_CHUNK.bit_length() - 1)]


def _hgrn_tables():
    ch = HGRN_CHUNK
    pair = 2 * ch
    t = np.arange(pair)
    col = np.arange(pair)[None, :]
    same_chunk = (col // ch) == (t[:, None] // ch)

    def prefix(idx):
        return (same_chunk & (col <= idx[:, None])).astype(np.float32)

    last = (t // ch) * ch + ch - 1
    blocks = [prefix(t), prefix(last) - prefix(t)]
    levels = _hgrn_levels()
    for h in levels:
        if h == 1:
            continue
        ref = (t & -(2 * h)) + h - 1
        sign = np.where((t & h) != 0, 1.0, -1.0)[:, None]
        blocks.append(sign * (prefix(t) - prefix(ref)))
    op = np.concatenate(blocks, axis=0)
    op = np.concatenate([op, op], axis=1)

    ti = np.arange(ch)[:, None]
    sj = np.arange(pair)[None, :] & (ch - 1)
    masks = [(ti == sj)]
    for h in levels:
        masks.append((((ti ^ sj) & -(2 * h)) == 0) & ((ti & h) != 0) & ((sj & h) == 0))
    masks.append((((ti ^ sj) & -(ch // 2)) == 0) & (sj <= ti))
    return jnp.asarray(op, BF16), jnp.asarray(np.stack(masks), F32)


def _hgrn_kernel(op_ref, mask_ref, q_ref, k_ref, v_ref, gate_ref, lf_ref, gn_ref, o_ref, st_ref):
    seq = q_ref.shape[1]
    ch = HGRN_CHUNK
    pair = 2 * ch
    st_ref[...] = jnp.zeros_like(st_ref)
    gain = gn_ref[...]
    halves = _hgrn_levels()
    first_chunk = lax.broadcasted_iota(jnp.int32, (ch, pair), 1) < ch
    row_id = lax.broadcasted_iota(jnp.int32, (pair, LANES), 0)
    odd_row = (row_id & 1) == 1
    first_half = (row_id & (ch // 2)) == 0
    zeros_half = jnp.zeros((ch, LANES), BF16)

    def side_by_side(x):
        return jnp.concatenate([x[:ch], x[ch:]], axis=1)

    def block_diag(x):
        return jnp.concatenate([jnp.concatenate([x[:ch], zeros_half], axis=1),
                                jnp.concatenate([zeros_half, x[ch:]], axis=1)], axis=0)

    group = min(HGRN_GROUP, seq // pair)

    def pair_rows(gi, c):
        return pl.ds(pl.multiple_of(gi * (group * pair), pair) + c * pair, pair)

    def chunk_bcast(b, row):
        return jnp.concatenate([jnp.broadcast_to(b[c0 + row:c0 + row + 1, :], (ch, LANES))
                                for c0 in (0, ch)], axis=0)

    def scores_any_decay(qb, kb, cum, g):
        attn = _dot_nt(side_by_side(qb), block_diag(kb)) * mask_ref[0]
        for li, h in enumerate(halves):
            if h == 1:
                e = jnp.where(odd_row, jnp.exp2(g), 1.0).astype(BF16)
            else:
                e = jnp.exp2(cum[(2 + li) * pair:(3 + li) * pair]).astype(BF16)
            attn = attn + _dot_nt(side_by_side(qb * e), block_diag(kb * e)) * mask_ref[1 + li]
        b = cum[:pair]
        return attn, qb * jnp.exp2(b).astype(BF16), kb * jnp.exp2(cum[pair:2 * pair]).astype(BF16)

    def scores_bounded_decay(qb, kb, b):
        mid = ch // 2
        d = b - chunk_bcast(b, mid - 1)
        e = jnp.exp2(-jnp.abs(d)).astype(BF16)
        q_top = qb * e
        attn = _dot_nt(side_by_side(q_top), block_diag(kb * e)) * mask_ref[1]
        q_dec = qb * jnp.exp2(b).astype(BF16)
        q_loc = jnp.where(first_half, q_dec, q_top)
        k_loc = kb * jnp.exp2(jnp.where(first_half, -b, -d)).astype(BF16)
        attn = attn + _dot_nt(side_by_side(q_loc), block_diag(k_loc)) * mask_ref[len(halves) + 1]
        k_dec = kb * jnp.exp2(chunk_bcast(b, ch - 1) - b).astype(BF16)
        return attn, q_dec, k_dec

    def decay_exponents(rows, bounded):
        op = op_ref[:pair, :] if bounded else op_ref[...]
        g = lf_ref[0, rows, :]
        g_hi = g.astype(BF16)
        g_lo = (g - g_hi.astype(F32)).astype(BF16)
        return g, _dot(op, jnp.concatenate([g_hi, g_lo], axis=0))

    def pair_scores(rows, g, cum, bounded):
        qb = q_ref[0, rows, :]
        kb = k_ref[0, rows, :]
        b = cum[:pair]
        if bounded:
            attn, q_dec, k_dec = scores_bounded_decay(qb, kb, b)
        else:
            attn, q_dec, k_dec = scores_any_decay(qb, kb, cum, g)
        cross = _dot_nt(q_dec[ch:], k_dec)
        lhs = jnp.concatenate([jnp.where(first_chunk, attn, 0.0),
                               jnp.where(first_chunk, cross, attn)], axis=0).astype(BF16)
        tot0 = jnp.exp2(b[ch - 1:ch, :])
        tot1 = jnp.exp2(b[pair - 1:pair, :])
        q_in = jnp.concatenate([q_dec[:ch], q_dec[ch:] * tot0.astype(BF16)], axis=0)
        k_up = jnp.concatenate([k_dec[:ch] * tot1.astype(BF16), k_dec[ch:]], axis=0)
        return lhs, q_in, k_up, tot0 * tot1

    def step(gi, carry, bounded):
        ahead_exp, ahead_sc = HGRN_AHEAD
        exps, scs = {}, {}
        st = st_ref[...]
        for tick in range(group + ahead_exp + ahead_sc):
            if tick < group:
                exps[tick] = decay_exponents(pair_rows(gi, tick), bounded)
            c = tick - ahead_exp
            if 0 <= c < group:
                scs[c] = pair_scores(pair_rows(gi, c), *exps.pop(c), bounded)
            c = tick - ahead_exp - ahead_sc
            if 0 <= c < group:
                rows = pair_rows(gi, c)
                v = v_ref[0, rows, :]
                lhs, q_in, k_up, total = scs.pop(c)
                o = _dot_nt(q_in, st.astype(BF16)) + _dot(lhs, v)
                st = st * total + _dot_tn(v, k_up)
                y = _rms(o, gain) * gate_ref[0, rows, :].astype(F32)
                o_ref[0, rows, :] = y.astype(BF16)
        st_ref[...] = st
        return carry

    half_rows = ch // 2
    half_tot = jnp.sum(lf_ref[0].reshape(seq // half_rows, half_rows, LANES), axis=1)
    worst = jnp.min(jnp.min(half_tot, axis=0, keepdims=True), axis=1, keepdims=True)
    bounded_ok = worst[0, 0] >= HGRN_MIN_HALF_LOG2
    n_steps = seq // (group * pair)

    @pl.when(bounded_ok)
    def _():
        lax.fori_loop(0, n_steps, functools.partial(step, bounded=True), 0)

    @pl.when(jnp.logical_not(bounded_ok))
    def _():
        lax.fori_loop(0, n_steps, functools.partial(step, bounded=False), 0)


def _hgrn_recurrence(proj, logf, gnorm):
    b, seq, _ = proj.shape
    h = HGRN_HEADS
    spec = lambda off: pl.BlockSpec((1, seq, LANES), lambda bi, hi: (bi, 0, off + hi))
    op, masks = _hgrn_tables()
    return pl.pallas_call(
        _hgrn_kernel,
        grid=(b, h),
        in_specs=[_resident(op.shape), _resident(masks.shape),
                  spec(0), spec(h), spec(2 * h), spec(3 * h), spec(0), _resident(gnorm.shape)],
        out_specs=pl.BlockSpec((1, seq, LANES), lambda bi, hi: (bi, 0, hi)),
        out_shape=jax.ShapeDtypeStruct((b, seq, h * HGRN_VAL_DIM), BF16),
        scratch_shapes=[pltpu.VMEM((HGRN_VAL_DIM, HGRN_KEY_DIM), F32)],
        compiler_params=_params(("parallel", "parallel")),
        name="hgrn_recurrence",
    )(op, masks, proj, proj, proj, proj, logf, gnorm)


def _post_kernel(*refs, n_mix, hidden, chunks):
    mix_refs = refs[:n_mix]
    wo_ref, x_ref, g_ref, win_ref, wout_ref, o_ref = refs[n_mix:]
    gains = g_ref[...]
    mixed = jnp.concatenate([r[...] for r in mix_refs], axis=-1) if n_mix > 1 else mix_refs[0][...]
    x1 = x_ref[...] + _rms(_dot(mixed, wo_ref[...].astype(BF16)), gains[1:2])
    h = _rms(x1, gains[2:3]).astype(BF16)
    acc = jnp.zeros(x1.shape, F32)
    for lo, hi in chunks:
        gate = _dot(h, win_ref[:, lo:hi].astype(BF16))
        up = _dot(h, win_ref[:, hidden + lo:hidden + hi].astype(BF16))
        act = (gate * _sigmoid(gate) * up).astype(BF16)
        acc = acc + _dot(act, wout_ref[lo:hi, :].astype(BF16))
    o_ref[...] = x1 + _rms(acc, gains[3:4])


def _post_mixer(mix_list, w_out, mixer_layer, x2d, gains, ffn_w_in, ffn_w_out, layer, seq):
    t, d = x2d.shape
    hidden = ffn_w_out.shape[1]
    tm = min(ROW_TILE, seq)
    chunks = tuple((lo, min(lo + COL_CHUNK, hidden)) for lo in range(0, hidden, COL_CHUNK))
    mix_specs = [pl.BlockSpec((tm, m.shape[1]), lambda i: (i, 0)) for m in mix_list]
    return pl.pallas_call(
        functools.partial(_post_kernel, n_mix=len(mix_list), hidden=hidden, chunks=chunks),
        grid=(t // tm,),
        in_specs=mix_specs + [_layer_block(w_out, mixer_layer),
                              pl.BlockSpec((tm, d), lambda i: (i, 0)),
                              _layer_block(gains, layer),
                              _layer_block(ffn_w_in, layer),
                              _layer_block(ffn_w_out, layer)],
        out_specs=pl.BlockSpec((tm, d), lambda i: (i, 0)),
        out_shape=jax.ShapeDtypeStruct((t, d), F32),
        compiler_params=_params(("parallel",)),
        name="post_mixer_ffn",
    )(*mix_list, w_out, x2d, gains, ffn_w_in, ffn_w_out)


def _rope_tables(seq):
    half = HEAD_DIM // 2
    inv_freq = ROPE_THETA ** (-jnp.arange(0, HEAD_DIM, 2, dtype=F32) / HEAD_DIM)
    ang = jnp.arange(seq, dtype=F32)[:, None] * inv_freq[None, :]
    reps = LANES // half
    sign = jnp.tile(jnp.concatenate([-jnp.ones((half,), F32), jnp.ones((half,), F32)]), LANES // HEAD_DIM)
    return jnp.tile(jnp.cos(ang), (1, reps)), jnp.tile(jnp.sin(ang), (1, reps)) * sign[None, :]


def kernel(x, norm_gains, att_w_in, att_lambda, att_subln, att_w_out, rec_w_in, rec_lower_bounds,
           rec_gnorm, rec_w_out, ffn_w_in, ffn_w_out):
    b, seq, d = x.shape
    depth = norm_gains.shape[0]
    cos, sin = _rope_tables(seq)
    x2d = x.reshape(b * seq, d)
    for layer in range(depth):
        gain_in = norm_gains[layer, 0:1]
        j = layer // 2
        if layer % 2 == 0:
            lambda_init = 0.8 - 0.6 * math.exp(-0.3 * layer)
            proj = _att_in_proj(x2d, gain_in, att_w_in, j, cos, sin, seq)
            proj = proj.reshape(b, seq, -1)
            oa = _dilated_attention(proj)
            od = _diff_attention(proj, att_lambda[j], att_subln[j][:, None], lambda_init)
            mix = [oa.reshape(b * seq, -1), od.reshape(b * seq, -1)]
            w_out = att_w_out
        else:
            proj, logf = _rec_in_proj(x2d, gain_in, rec_w_in, rec_lower_bounds, j, seq)
            o = _hgrn_recurrence(proj.reshape(b, seq, -1), logf.reshape(b, seq, -1), rec_gnorm[j][None, :])
            mix = [o.reshape(b * seq, -1)]
            w_out = rec_w_out
        x2d = _post_mixer(mix, w_out, j, x2d, norm_gains, ffn_w_in, ffn_w_out, layer, seq)
    return x2d.reshape(b, seq, d)
```

```python
import functools
import math

import jax
import jax.numpy as jnp
import numpy as np
from jax import lax
from jax.experimental import pallas as pl
from jax.experimental.pallas import tpu as pltpu

F32 = jnp.float32
BF16 = jnp.bfloat16

HEAD_DIM = 64
A_HEADS = 8
A_WIDTH = A_HEADS * HEAD_DIM
DILATED_PATTERNS = ((128, 1), (512, 4), (2048, 16))
DIL_BLOCK = 128
DIFF_HEADS = 4
DIFF_QK_WIDTH = 2 * DIFF_HEADS * HEAD_DIM
DIFF_V_DIM = 2 * HEAD_DIM
DIFF_V_WIDTH = DIFF_HEADS * DIFF_V_DIM
ROPE_THETA = 10000.0
HGRN_HEADS = 8
HGRN_KEY_DIM = 128
HGRN_VAL_DIM = 128
NORM_EPS = 1e-6
NEG_INF = -1e30
LOG2_E = math.log2(math.e)

LANES = 128
BF16_ROWS = 16
V7X_VMEM_BYTES = 64 * 1024 * 1024
VMEM_LIMIT = V7X_VMEM_BYTES * 7 // 8

ROW_TILE = 512
IN_ROW_TILE = 1024
COL_CHUNK = 512
DIL_GROUP = 8
DIFF_Q_TILE = 256
DIFF_LOOKAHEAD = 2
HGRN_CHUNK = 64
HGRN_GROUP = 16
HGRN_AHEAD = (2, 2)
HGRN_MIN_HALF_LOG2 = -96.0


def _rms(x, gain):
    ms = jnp.mean(x * x, axis=-1, keepdims=True)
    return x * lax.rsqrt(ms + NORM_EPS) * gain


def _sigmoid(x):
    return 1.0 / (1.0 + jnp.exp(-x))


def _dot(a, b):
    return jnp.dot(a, b, preferred_element_type=F32)


def _dot_nt(a, b):
    return lax.dot_general(a, b, (((1,), (1,)), ((), ())), preferred_element_type=F32)


def _dot_tn(a, b):
    return lax.dot_general(a, b, (((0,), (0,)), ((), ())), preferred_element_type=F32)


def _params(semantics):
    return pltpu.CompilerParams(dimension_semantics=semantics, vmem_limit_bytes=VMEM_LIMIT)


def _resident(shape):
    return pl.BlockSpec(shape, lambda *_: (0,) * len(shape), pipeline_mode=pl.Buffered(1))


def _layer_block(stack, layer):
    tail = stack.shape[1:]
    return pl.BlockSpec((None,) + tail, lambda *_: (layer,) + (0,) * len(tail), pipeline_mode=pl.Buffered(1))


def _att_in_kernel(x_ref, g_ref, w_ref, cos_ref, sin_ref, o_ref):
    h = _rms(x_ref[...], g_ref[...]).astype(BF16)
    tm = h.shape[0]
    cos = cos_ref[...]
    sin = sin_ref[...]
    lane = lax.broadcasted_iota(jnp.int32, (tm, LANES), 1)
    first_half = (lane % HEAD_DIM) < (HEAD_DIM // 2)
    n_cols = w_ref.shape[1]
    q_scale = HEAD_DIM ** -0.5 * LOG2_E
    rope_hi = 2 * A_WIDTH
    diff_lo = 3 * A_WIDTH
    diff_rope_hi = diff_lo + 2 * DIFF_QK_WIDTH
    for c0 in range(0, n_cols, COL_CHUNK):
        acc = _dot(h, w_ref[:, c0:c0 + COL_CHUNK].astype(BF16))
        for l0 in range(0, COL_CHUNK, LANES):
            col = c0 + l0
            y = acc[:, l0:l0 + LANES]
            if col < rope_hi or diff_lo <= col < diff_rope_hi:
                rot = jnp.where(first_half, pltpu.roll(y, LANES - HEAD_DIM // 2, 1),
                                pltpu.roll(y, HEAD_DIM // 2, 1))
                y = y * cos + rot * sin
                if col < A_WIDTH or diff_lo <= col < diff_lo + DIFF_QK_WIDTH:
                    y = y * q_scale
            o_ref[:, col:col + LANES] = y.astype(BF16)


def _att_in_proj(x2d, gain, w_stack, layer, cos, sin, seq):
    t, d = x2d.shape
    n = w_stack.shape[2]
    tm = min(IN_ROW_TILE, seq)
    tiles_per_seq = seq // tm
    return pl.pallas_call(
        _att_in_kernel,
        grid=(t // tm,),
        in_specs=[pl.BlockSpec((tm, d), lambda i: (i, 0)),
                  _resident((1, d)),
                  _layer_block(w_stack, layer),
                  pl.BlockSpec((tm, LANES), lambda i: (i % tiles_per_seq, 0)),
                  pl.BlockSpec((tm, LANES), lambda i: (i % tiles_per_seq, 0))],
        out_specs=pl.BlockSpec((tm, n), lambda i: (i, 0)),
        out_shape=jax.ShapeDtypeStruct((t, n), BF16),
        compiler_params=_params(("parallel",)),
        name="att_in_proj",
    )(x2d, gain, w_stack, cos, sin)


def _dilated_kernel(q_ref, k_ref, v_ref, o_ref, qf, kf, vf, acc, mx, den):
    seq = q_ref.shape[1]
    blk = DIL_BLOCK
    qf[...] = q_ref[0].astype(F32)
    kf[...] = k_ref[0].astype(F32)
    vf[...] = v_ref[0].astype(F32)

    lane = lax.broadcasted_iota(jnp.int32, (blk, LANES), 1)
    head_a = lane < HEAD_DIM
    qi = lax.broadcasted_iota(jnp.int32, (2 * blk, 2 * blk), 0) % blk
    ci = lax.broadcasted_iota(jnp.int32, (2 * blk, 2 * blk), 1)
    mask_two = (ci >= qi) & (ci <= qi + blk)
    qi1 = lax.broadcasted_iota(jnp.int32, (2 * blk, blk), 0) % blk
    ci1 = lax.broadcasted_iota(jnp.int32, (2 * blk, blk), 1)
    mask_one = ci1 <= qi1

    zero = jnp.zeros((blk, LANES), BF16)

    def attend(p, blocks):
        scores = []
        for qb, kc, _, _, _ in blocks:
            q2 = jnp.concatenate([jnp.where(head_a, qb, zero), jnp.where(head_a, zero, qb)], axis=0)
            scores.append(_dot_nt(q2, kc))
        probs = []
        for s, (_, _, _, first, _) in zip(scores, blocks):
            s = jnp.where(mask_one if first else mask_two, s, NEG_INF)
            m = jnp.max(s, axis=-1, keepdims=True)
            probs.append((jnp.exp2(s - m).astype(BF16), m))
        for (e, m), (_, _, vc, _, rows) in zip(probs, blocks):
            o = _dot(e, jnp.concatenate([vc, jnp.ones_like(vc)], axis=1))
            acc[p, rows, :] = jnp.where(head_a, o[:blk, :LANES], o[blk:, :LANES])
            mx[p, rows, :] = jnp.where(head_a, m[:blk], m[blk:])
            den[p, rows, :] = jnp.where(head_a, o[:blk, LANES:], o[blk:, LANES:])

    for p, (window, dil) in enumerate(DILATED_PATTERNS):
        assert window // dil == blk
        length = seq // dil
        nb = length // blk

        if dil == 1:
            for n0 in range(0, nb, DIL_GROUP):
                blocks = []
                for n in range(n0, min(n0 + DIL_GROUP, nb)):
                    keys = slice(max(n - 1, 0) * blk, (n + 1) * blk)
                    blocks.append((q_ref[0, n * blk:(n + 1) * blk, :], k_ref[0, keys, :], v_ref[0, keys, :],
                                   n == 0, pl.ds(n * blk, blk)))
                attend(p, blocks)
            continue

        per_step = max(DIL_GROUP // nb, 1)

        def residues(step, carry, p=p, dil=dil, length=length, nb=nb, per_step=per_step):
            blocks = []
            for j in range(per_step):
                r = step * per_step + j
                qr = qf[pl.ds(r, length, stride=dil), :].astype(BF16)
                kr = kf[pl.ds(r, length, stride=dil), :].astype(BF16)
                vr = vf[pl.ds(r, length, stride=dil), :].astype(BF16)
                for n in range(nb):
                    keys = slice(max(n - 1, 0) * blk, (n + 1) * blk)
                    blocks.append((qr[n * blk:(n + 1) * blk], kr[keys], vr[keys], n == 0,
                                   pl.ds(r + n * blk * dil, blk, stride=dil)))
            attend(p, blocks)
            return carry

        lax.fori_loop(0, dil // per_step, residues, 0)

    m_all = jnp.maximum(jnp.maximum(mx[0], mx[1]), mx[2])
    num = jnp.zeros((seq, LANES), F32)
    tot = jnp.zeros((seq, LANES), F32)
    for p in range(len(DILATED_PATTERNS)):
        w = jnp.exp2(mx[p] - m_all)
        num = num + w * acc[p]
        tot = tot + w * den[p]
    o_ref[0] = (num / tot).astype(BF16)


def _dilated_attention(proj):
    b, seq, _ = proj.shape
    n_pairs = A_WIDTH // LANES
    spec = lambda off: pl.BlockSpec((1, seq, LANES), lambda bi, pi: (bi, 0, off + pi))
    n_pat = len(DILATED_PATTERNS)
    return pl.pallas_call(
        _dilated_kernel,
        grid=(b, n_pairs),
        in_specs=[spec(0), spec(n_pairs), spec(2 * n_pairs)],
        out_specs=pl.BlockSpec((1, seq, LANES), lambda bi, pi: (bi, 0, pi)),
        out_shape=jax.ShapeDtypeStruct((b, seq, A_WIDTH), BF16),
        scratch_shapes=[pltpu.VMEM((seq, LANES), F32)] * 3
        + [pltpu.VMEM((n_pat, seq, LANES), F32)] * 3,
        compiler_params=_params(("parallel", "parallel")),
        name="dilated_attention",
    )(proj, proj, proj)


def _diff_kernel(lam_ref, q_ref, k_ref, v_ref, sg_ref, o_ref, vt_ref, *, lambda_init):
    lp = lam_ref[...]
    l1 = jnp.sum(lp[0:1] * lp[1:2], axis=-1, keepdims=True)
    l2 = jnp.sum(lp[2:3] * lp[3:4], axis=-1, keepdims=True)
    lam = jnp.exp(l1) - jnp.exp(l2) + lambda_init
    seq = q_ref.shape[1]
    tq = min(DIFF_Q_TILE, seq)
    lane = lax.broadcasted_iota(jnp.int32, (tq, LANES), 1)
    sub0 = lane < HEAD_DIM
    zero = jnp.zeros((tq, LANES), BF16)
    gain = sg_ref[...] * (1.0 - lambda_init)

    vt_ref[0:LANES, :] = v_ref[0].astype(F32).T.astype(BF16)
    pad_rows = vt_ref.shape[0] - LANES
    vt_ref[LANES:, :] = jnp.where(lax.broadcasted_iota(jnp.int32, (pad_rows, seq), 0) == 0, 1.0, 0.0).astype(BF16)

    causal = (lax.broadcasted_iota(jnp.int32, (tq, 2 * tq), 0)
              <= lax.broadcasted_iota(jnp.int32, (tq, 2 * tq), 1) % tq)

    def scores(i):
        q = q_ref[0, i * tq:(i + 1) * tq, :]
        q2 = jnp.concatenate([jnp.where(sub0, q, zero), jnp.where(sub0, zero, q)], axis=0)
        past = i * tq
        s_diag = jnp.where(causal, _dot_nt(k_ref[0, past:past + tq, :], q2), NEG_INF)
        s_past = _dot_nt(k_ref[0, 0:past, :], q2) if past else None
        return s_diag, s_past

    n_tiles = seq // tq
    ahead = [scores(i) for i in range(min(DIFF_LOOKAHEAD, n_tiles))]
    for i in range(n_tiles):
        s_diag, s_past = ahead.pop(0)
        if i + DIFF_LOOKAHEAD < n_tiles:
            ahead.append(scores(i + DIFF_LOOKAHEAD))
        past = i * tq
        m = jnp.max(s_diag, axis=0, keepdims=True)
        if past:
            m = jnp.maximum(m, jnp.max(s_past, axis=0, keepdims=True))
        o = _dot(vt_ref[:, past:past + tq], jnp.exp2(s_diag - m).astype(BF16))
        if past:
            o = o + _dot(vt_ref[:, 0:past], jnp.exp2(s_past - m).astype(BF16))
        o = o[:LANES] / o[LANES:LANES + 1]
        w = o[:, :tq] - lam * o[:, tq:]
        ms = jnp.mean(w * w, axis=0, keepdims=True)
        y = w * lax.rsqrt(ms + NORM_EPS) * gain
        o_ref[0, i * tq:(i + 1) * tq, :] = y.T.astype(BF16)


def _diff_attention(proj, lam_params, subln, lambda_init):
    b, seq, _ = proj.shape
    q_off = 3 * A_WIDTH // LANES
    k_off = q_off + DIFF_QK_WIDTH // LANES
    v_off = k_off + DIFF_QK_WIDTH // LANES
    spec = lambda off: pl.BlockSpec((1, seq, LANES), lambda bi, hi: (bi, 0, off + hi))
    return pl.pallas_call(
        functools.partial(_diff_kernel, lambda_init=lambda_init),
        grid=(b, DIFF_HEADS),
        in_specs=[_resident(lam_params.shape), spec(q_off), spec(k_off), spec(v_off),
                  _resident(subln.shape)],
        out_specs=pl.BlockSpec((1, seq, LANES), lambda bi, hi: (bi, 0, hi)),
        out_shape=jax.ShapeDtypeStruct((b, seq, DIFF_V_WIDTH), BF16),
        scratch_shapes=[pltpu.VMEM((DIFF_V_DIM + BF16_ROWS, seq), BF16)],
        compiler_params=_params(("parallel", "parallel")),
        name="diff_attention",
    )(lam_params, proj, proj, proj, subln)


def _rec_in_kernel(x_ref, g_ref, w_ref, lb_ref, o_ref, lf_ref, *, layer):
    h = _rms(x_ref[...], g_ref[...]).astype(BF16)
    lb_raw = lb_ref[...]
    lb_e = jnp.exp(lb_raw - jnp.max(lb_raw, axis=0, keepdims=True))
    lb_p = lb_e / jnp.sum(lb_e, axis=0, keepdims=True)
    lower = jnp.sum(lb_p[0:layer + 1], axis=0, keepdims=True) - lb_p[0:1]
    width = lf_ref.shape[1]
    n_cols = w_ref.shape[1]
    q_scale = HGRN_KEY_DIM ** -0.5
    starts = sorted(range(0, n_cols, COL_CHUNK), key=lambda c: (c // width == 2, c))
    for c0 in starts:
        y = _dot(h, w_ref[:, c0:c0 + COL_CHUNK].astype(BF16))
        group = c0 // width
        if group == 0:
            o_ref[:, c0:c0 + COL_CHUNK] = (y * _sigmoid(y) * q_scale).astype(BF16)
        elif group == 1:
            f0 = c0 - width
            lo = lower[:, f0:f0 + COL_CHUNK]
            forget = lo + (1.0 - lo) * _sigmoid(y)
            lf_ref[:, f0:f0 + COL_CHUNK] = jnp.log2(forget)
            o_ref[:, c0:c0 + COL_CHUNK] = (1.0 - forget).astype(BF16)
        elif group == 2:
            o_ref[:, c0:c0 + COL_CHUNK] = y.astype(BF16)
        else:
            o_ref[:, c0:c0 + COL_CHUNK] = (y * _sigmoid(y)).astype(BF16)


def _rec_in_proj(x2d, gain, w_stack, lower_bounds, layer, seq):
    t, d = x2d.shape
    n = w_stack.shape[2]
    width = lower_bounds.shape[1]
    tm = min(IN_ROW_TILE, seq)
    return pl.pallas_call(
        functools.partial(_rec_in_kernel, layer=layer),
        grid=(t // tm,),
        in_specs=[pl.BlockSpec((tm, d), lambda i: (i, 0)),
                  _resident((1, d)),
                  _layer_block(w_stack, layer),
                  _resident(lower_bounds.shape)],
        out_specs=[pl.BlockSpec((tm, n), lambda i: (i, 0)),
                   pl.BlockSpec((tm, width), lambda i: (i, 0))],
        out_shape=[jax.ShapeDtypeStruct((t, n), BF16),
                   jax.ShapeDtypeStruct((t, width), F32)],
        compiler_params=_params(("parallel",)),
        name="rec_in_proj",
    )(x2d, gain, w_stack, lower_bounds)


def _hgrn_levels():
    return [HGRN_CHUNK >> (i + 1) for i in range(HGRN_CHUNK.bit_length() - 1)]


def _hgrn_tables():
    ch = HGRN_CHUNK
    pair = 2 * ch
    t = np.arange(pair)
    col = np.arange(pair)[None, :]
    same_chunk = (col // ch) == (t[:, None] // ch)

    def prefix(idx):
        return (same_chunk & (col <= idx[:, None])).astype(np.float32)

    last = (t // ch) * ch + ch - 1
    blocks = [prefix(t), prefix(last) - prefix(t)]
    levels = _hgrn_levels()
    for h in levels:
        if h == 1:
            continue
        ref = (t & -(2 * h)) + h - 1
        sign = np.where((t & h) != 0, 1.0, -1.0)[:, None]
        blocks.append(sign * (prefix(t) - prefix(ref)))
    op = np.concatenate(blocks, axis=0)
    op = np.concatenate([op, op], axis=1)

    ti = np.arange(ch)[:, None]
    sj = np.arange(pair)[None, :] & (ch - 1)
    masks = [(ti == sj)]
    for h in levels:
        masks.append((((ti ^ sj) & -(2 * h)) == 0) & ((ti & h) != 0) & ((sj & h) == 0))
    masks.append((((ti ^ sj) & -(ch // 2)) == 0) & (sj <= ti))
    return jnp.asarray(op, BF16), jnp.asarray(np.stack(masks), F32)


def _hgrn_kernel(op_ref, mask_ref, q_ref, k_ref, v_ref, gate_ref, lf_ref, gn_ref, o_ref, st_ref):
    seq = q_ref.shape[1]
    ch = HGRN_CHUNK
    pair = 2 * ch
    st_ref[...] = jnp.zeros_like(st_ref)
    gain = gn_ref[...]
    halves = _hgrn_levels()
    first_chunk = lax.broadcasted_iota(jnp.int32, (ch, pair), 1) < ch
    row_id = lax.broadcasted_iota(jnp.int32, (pair, LANES), 0)
    odd_row = (row_id & 1) == 1
    first_half = (row_id & (ch // 2)) == 0
    zeros_half = jnp.zeros((ch, LANES), BF16)

    def side_by_side(x):
        return jnp.concatenate([x[:ch], x[ch:]], axis=1)

    def block_diag(x):
        return jnp.concatenate([jnp.concatenate([x[:ch], zeros_half], axis=1),
                                jnp.concatenate([zeros_half, x[ch:]], axis=1)], axis=0)

    group = min(HGRN_GROUP, seq // pair)

    def pair_rows(gi, c):
        return pl.ds(pl.multiple_of(gi * (group * pair), pair) + c * pair, pair)

    def chunk_bcast(b, row):
        return jnp.concatenate([jnp.broadcast_to(b[c0 + row:c0 + row + 1, :], (ch, LANES))
                                for c0 in (0, ch)], axis=0)

    def scores_any_decay(qb, kb, cum, g):
        attn = _dot_nt(side_by_side(qb), block_diag(kb)) * mask_ref[0]
        for li, h in enumerate(halves):
            if h == 1:
                e = jnp.where(odd_row, jnp.exp2(g), 1.0).astype(BF16)
            else:
                e = jnp.exp2(cum[(2 + li) * pair:(3 + li) * pair]).astype(BF16)
            attn = attn + _dot_nt(side_by_side(qb * e), block_diag(kb * e)) * mask_ref[1 + li]
        b = cum[:pair]
        return attn, qb * jnp.exp2(b).astype(BF16), kb * jnp.exp2(cum[pair:2 * pair]).astype(BF16)

    def scores_bounded_decay(qb, kb, b):
        mid = ch // 2
        d = b - chunk_bcast(b, mid - 1)
        e = jnp.exp2(-jnp.abs(d)).astype(BF16)
        q_top = qb * e
        attn = _dot_nt(side_by_side(q_top), block_diag(kb * e)) * mask_ref[1]
        q_dec = qb * jnp.exp2(b).astype(BF16)
        q_loc = jnp.where(first_half, q_dec, q_top)
        k_loc = kb * jnp.exp2(jnp.where(first_half, -b, -d)).astype(BF16)
        attn = attn + _dot_nt(side_by_side(q_loc), block_diag(k_loc)) * mask_ref[len(halves) + 1]
        k_dec = kb * jnp.exp2(chunk_bcast(b, ch - 1) - b).astype(BF16)
        return attn, q_dec, k_dec

    def decay_exponents(rows, bounded):
        op = op_ref[:pair, :] if bounded else op_ref[...]
        g = lf_ref[0, rows, :]
        g_hi = g.astype(BF16)
        g_lo = (g - g_hi.astype(F32)).astype(BF16)
        return g, _dot(op, jnp.concatenate([g_hi, g_lo], axis=0))

    def pair_scores(rows, g, cum, bounded):
        qb = q_ref[0, rows, :]
        kb = k_ref[0, rows, :]
        b = cum[:pair]
        if bounded:
            attn, q_dec, k_dec = scores_bounded_decay(qb, kb, b)
        else:
            attn, q_dec, k_dec = scores_any_decay(qb, kb, cum, g)
        cross = _dot_nt(q_dec[ch:], k_dec)
        lhs = jnp.concatenate([jnp.where(first_chunk, attn, 0.0),
                               jnp.where(first_chunk, cross, attn)], axis=0).astype(BF16)
        tot0 = jnp.exp2(b[ch - 1:ch, :])
        tot1 = jnp.exp2(b[pair - 1:pair, :])
        q_in = jnp.concatenate([q_dec[:ch], q_dec[ch:] * tot0.astype(BF16)], axis=0)
        k_up = jnp.concatenate([k_dec[:ch] * tot1.astype(BF16), k_dec[ch:]], axis=0)
        return lhs, q_in, k_up, tot0 * tot1

    def step(gi, carry, bounded):
        ahead_exp, ahead_sc = HGRN_AHEAD
        exps, scs = {}, {}
        st = st_ref[...]
        for tick in range(group + ahead_exp + ahead_sc):
            if tick < group:
                exps[tick] = decay_exponents(pair_rows(gi, tick), bounded)
            c = tick - ahead_exp
            if 0 <= c < group:
                scs[c] = pair_scores(pair_rows(gi, c), *exps.pop(c), bounded)
            c = tick - ahead_exp - ahead_sc
            if 0 <= c < group:
                rows = pair_rows(gi, c)
                v = v_ref[0, rows, :]
                lhs, q_in, k_up, total = scs.pop(c)
                o = _dot_nt(q_in, st.astype(BF16)) + _dot(lhs, v)
                st = st * total + _dot_tn(v, k_up)
                y = _rms(o, gain) * gate_ref[0, rows, :].astype(F32)
                o_ref[0, rows, :] = y.astype(BF16)
        st_ref[...] = st
        return carry

    half_rows = ch // 2
    half_tot = jnp.sum(lf_ref[0].reshape(seq // half_rows, half_rows, LANES), axis=1)
    worst = jnp.min(jnp.min(half_tot, axis=0, keepdims=True), axis=1, keepdims=True)
    bounded_ok = worst[0, 0] >= HGRN_MIN_HALF_LOG2
    n_steps = seq // (group * pair)

    @pl.when(bounded_ok)
    def _():
        lax.fori_loop(0, n_steps, functools.partial(step, bounded=True), 0)

    @pl.when(jnp.logical_not(bounded_ok))
    def _():
        lax.fori_loop(0, n_steps, functools.partial(step, bounded=False), 0)


def _hgrn_recurrence(proj, logf, gnorm):
    b, seq, _ = proj.shape
    h = HGRN_HEADS
    spec = lambda off: pl.BlockSpec((1, seq, LANES), lambda bi, hi: (bi, 0, off + hi))
    op, masks = _hgrn_tables()
    return pl.pallas_call(
        _hgrn_kernel,
        grid=(b, h),
        in_specs=[_resident(op.shape), _resident(masks.shape),
                  spec(0), spec(h), spec(2 * h), spec(3 * h), spec(0), _resident(gnorm.shape)],
        out_specs=pl.BlockSpec((1, seq, LANES), lambda bi, hi: (bi, 0, hi)),
        out_shape=jax.ShapeDtypeStruct((b, seq, h * HGRN_VAL_DIM), BF16),
        scratch_shapes=[pltpu.VMEM((HGRN_VAL_DIM, HGRN_KEY_DIM), F32)],
        compiler_params=_params(("parallel", "parallel")),
        name="hgrn_recurrence",
    )(op, masks, proj, proj, proj, proj, logf, gnorm)


def _post_kernel(*refs, n_mix, hidden, chunks):
    mix_refs = refs[:n_mix]
    wo_ref, x_ref, g_ref, win_ref, wout_ref, o_ref = refs[n_mix:]
    gains = g_ref[...]
    mixed = jnp.concatenate([r[...] for r in mix_refs], axis=-1) if n_mix > 1 else mix_refs[0][...]
    x1 = x_ref[...] + _rms(_dot(mixed, wo_ref[...].astype(BF16)), gains[1:2])
    h = _rms(x1, gains[2:3]).astype(BF16)
    acc = jnp.zeros(x1.shape, F32)
    for lo, hi in chunks:
        gate = _dot(h, win_ref[:, lo:hi].astype(BF16))
        up = _dot(h, win_ref[:, hidden + lo:hidden + hi].astype(BF16))
        act = (gate * _sigmoid(gate) * up).astype(BF16)
        acc = acc + _dot(act, wout_ref[lo:hi, :].astype(BF16))
    o_ref[...] = x1 + _rms(acc, gains[3:4])


def _post_mixer(mix_list, w_out, mixer_layer, x2d, gains, ffn_w_in, ffn_w_out, layer, seq):
    t, d = x2d.shape
    hidden = ffn_w_out.shape[1]
    tm = min(ROW_TILE, seq)
    chunks = tuple((lo, min(lo + COL_CHUNK, hidden)) for lo in range(0, hidden, COL_CHUNK))
    mix_specs = [pl.BlockSpec((tm, m.shape[1]), lambda i: (i, 0)) for m in mix_list]
    return pl.pallas_call(
        functools.partial(_post_kernel, n_mix=len(mix_list), hidden=hidden, chunks=chunks),
        grid=(t // tm,),
        in_specs=mix_specs + [_layer_block(w_out, mixer_layer),
                              pl.BlockSpec((tm, d), lambda i: (i, 0)),
                              _layer_block(gains, layer),
                              _layer_block(ffn_w_in, layer),
                              _layer_block(ffn_w_out, layer)],
        out_specs=pl.BlockSpec((tm, d), lambda i: (i, 0)),
        out_shape=jax.ShapeDtypeStruct((t, d), F32),
        compiler_params=_params(("parallel",)),
        name="post_mixer_ffn",
    )(*mix_list, w_out, x2d, gains, ffn_w_in, ffn_w_out)


def _rope_tables(seq):
    half = HEAD_DIM // 2
    inv_freq = ROPE_THETA ** (-jnp.arange(0, HEAD_DIM, 2, dtype=F32) / HEAD_DIM)
    ang = jnp.arange(seq, dtype=F32)[:, None] * inv_freq[None, :]
    reps = LANES // half
    sign = jnp.tile(jnp.concatenate([-jnp.ones((half,), F32), jnp.ones((half,), F32)]), LANES // HEAD_DIM)
    return jnp.tile(jnp.cos(ang), (1, reps)), jnp.tile(jnp.sin(ang), (1, reps)) * sign[None, :]


def kernel(x, norm_gains, att_w_in, att_lambda, att_subln, att_w_out, rec_w_in, rec_lower_bounds,
           rec_gnorm, rec_w_out, ffn_w_in, ffn_w_out):
    b, seq, d = x.shape
    depth = norm_gains.shape[0]
    cos, sin = _rope_tables(seq)
    x2d = x.reshape(b * seq, d)
    for layer in range(depth):
        gain_in = norm_gains[layer, 0:1]
        j = layer // 2
        if layer % 2 == 0:
            lambda_init = 0.8 - 0.6 * math.exp(-0.3 * layer)
            proj = _att_in_proj(x2d, gain_in, att_w_in, j, cos, sin, seq)
            proj = proj.reshape(b, seq, -1)
            oa = _dilated_attention(proj)
            od = _diff_attention(proj, att_lambda[j], att_subln[j][:, None], lambda_init)
            mix = [oa.reshape(b * seq, -1), od.reshape(b * seq, -1)]
            w_out = att_w_out
        else:
            proj, logf = _rec_in_proj(x2d, gain_in, rec_w_in, rec_lower_bounds, j, seq)
            o = _hgrn_recurrence(proj.reshape(b, seq, -1), logf.reshape(b, seq, -1), rec_gnorm[j][None, :])
            mix = [o.reshape(b * seq, -1)]
            w_out = rec_w_out
        x2d = _post_mixer(mix, w_out, j, x2d, norm_gains, ffn_w_in, ffn_w_out, layer, seq)
    return x2d.reshape(b, seq, d)
```

```python
import functools
import math

import jax
import jax.numpy as jnp
import numpy as np
from jax import lax
from jax.experimental import pallas as pl
from jax.experimental.pallas import tpu as pltpu

F32 = jnp.float32
BF16 = jnp.bfloat16

HEAD_DIM = 64
A_HEADS = 8
A_WIDTH = A_HEADS * HEAD_DIM
DILATED_PATTERNS = ((128, 1), (512, 4), (2048, 16))
DIL_BLOCK = 128
DIFF_HEADS = 4
DIFF_QK_WIDTH = 2 * DIFF_HEADS * HEAD_DIM
DIFF_V_DIM = 2 * HEAD_DIM
DIFF_V_WIDTH = DIFF_HEADS * DIFF_V_DIM
ROPE_THETA = 10000.0
HGRN_HEADS = 8
HGRN_KEY_DIM = 128
HGRN_VAL_DIM = 128
NORM_EPS = 1e-6
NEG_INF = -1e30
LOG2_E = math.log2(math.e)

LANES = 128
BF16_ROWS = 16
V7X_VMEM_BYTES = 64 * 1024 * 1024
VMEM_LIMIT = V7X_VMEM_BYTES * 7 // 8

ROW_TILE = 512
IN_ROW_TILE = 1024
COL_CHUNK = 512
DIL_GROUP = 8
DIFF_Q_TILE = 256
DIFF_LOOKAHEAD = 2
HGRN_CHUNK = 64
HGRN_GROUP = 16
HGRN_AHEAD = (2, 2)
HGRN_MIN_HALF_LOG2 = -96.0


def _rms(x, gain):
    ms = jnp.mean(x * x, axis=-1, keepdims=True)
    return x * lax.rsqrt(ms + NORM_EPS) * gain


def _sigmoid(x):
    return 1.0 / (1.0 + jnp.exp(-x))


def _silu(x):
    return x * (0.5 * jnp.tanh(0.5 * x) + 0.5)


def _dot(a, b):
    return jnp.dot(a, b, preferred_element_type=F32)


def _dot_nt(a, b):
    return lax.dot_general(a, b, (((1,), (1,)), ((), ())), preferred_element_type=F32)


def _dot_tn(a, b):
    return lax.dot_general(a, b, (((0,), (0,)), ((), ())), preferred_element_type=F32)


def _params(semantics):
    return pltpu.CompilerParams(dimension_semantics=semantics, vmem_limit_bytes=VMEM_LIMIT)


def _resident(shape):
    return pl.BlockSpec(shape, lambda *_: (0,) * len(shape), pipeline_mode=pl.Buffered(1))


def _layer_block(stack, layer):
    tail = stack.shape[1:]
    return pl.BlockSpec((None,) + tail, lambda *_: (layer,) + (0,) * len(tail), pipeline_mode=pl.Buffered(1))


def _att_in_kernel(x_ref, g_ref, w_ref, cos_ref, sin_ref, o_ref):
    h = _rms(x_ref[...], g_ref[...]).astype(BF16)
    tm = h.shape[0]
    cos = cos_ref[...]
    sin = sin_ref[...]
    lane = lax.broadcasted_iota(jnp.int32, (tm, LANES), 1)
    first_half = (lane % HEAD_DIM) < (HEAD_DIM // 2)
    n_cols = w_ref.shape[1]
    q_scale = HEAD_DIM ** -0.5 * LOG2_E
    rope_hi = 2 * A_WIDTH
    diff_lo = 3 * A_WIDTH
    diff_rope_hi = diff_lo + 2 * DIFF_QK_WIDTH
    for c0 in range(0, n_cols, COL_CHUNK):
        acc = _dot(h, w_ref[:, c0:c0 + COL_CHUNK].astype(BF16))
        for l0 in range(0, COL_CHUNK, LANES):
            col = c0 + l0
            y = acc[:, l0:l0 + LANES]
            if col < rope_hi or diff_lo <= col < diff_rope_hi:
                rot = jnp.where(first_half, pltpu.roll(y, LANES - HEAD_DIM // 2, 1),
                                pltpu.roll(y, HEAD_DIM // 2, 1))
                y = y * cos + rot * sin
                if col < A_WIDTH or diff_lo <= col < diff_lo + DIFF_QK_WIDTH:
                    y = y * q_scale
            o_ref[:, col:col + LANES] = y.astype(BF16)


def _att_in_proj(x2d, gain, w_stack, layer, cos, sin, seq):
    t, d = x2d.shape
    n = w_stack.shape[2]
    tm = min(IN_ROW_TILE, seq)
    tiles_per_seq = seq // tm
    return pl.pallas_call(
        _att_in_kernel,
        grid=(t // tm,),
        in_specs=[pl.BlockSpec((tm, d), lambda i: (i, 0)),
                  _resident((1, d)),
                  _layer_block(w_stack, layer),
                  pl.BlockSpec((tm, LANES), lambda i: (i % tiles_per_seq, 0)),
                  pl.BlockSpec((tm, LANES), lambda i: (i % tiles_per_seq, 0))],
        out_specs=pl.BlockSpec((tm, n), lambda i: (i, 0)),
        out_shape=jax.ShapeDtypeStruct((t, n), BF16),
        compiler_params=_params(("parallel",)),
        name="att_in_proj",
    )(x2d, gain, w_stack, cos, sin)


def _dilated_kernel(q_ref, k_ref, v_ref, o_ref, qf, kf, vf, acc, mx, den):
    seq = q_ref.shape[1]
    blk = DIL_BLOCK
    qf[...] = q_ref[0].astype(F32)
    kf[...] = k_ref[0].astype(F32)
    vf[...] = v_ref[0].astype(F32)

    lane = lax.broadcasted_iota(jnp.int32, (blk, LANES), 1)
    head_a = lane < HEAD_DIM
    qi = lax.broadcasted_iota(jnp.int32, (2 * blk, 2 * blk), 0) % blk
    ci = lax.broadcasted_iota(jnp.int32, (2 * blk, 2 * blk), 1)
    mask_two = (ci >= qi) & (ci <= qi + blk)
    qi1 = lax.broadcasted_iota(jnp.int32, (2 * blk, blk), 0) % blk
    ci1 = lax.broadcasted_iota(jnp.int32, (2 * blk, blk), 1)
    mask_one = ci1 <= qi1

    zero = jnp.zeros((blk, LANES), BF16)

    def attend(p, blocks):
        scores = []
        for qb, kc, _, _, _ in blocks:
            q2 = jnp.concatenate([jnp.where(head_a, qb, zero), jnp.where(head_a, zero, qb)], axis=0)
            scores.append(_dot_nt(q2, kc))
        probs = []
        for s, (_, _, _, first, _) in zip(scores, blocks):
            s = jnp.where(mask_one if first else mask_two, s, NEG_INF)
            m = jnp.max(s, axis=-1, keepdims=True)
            probs.append((jnp.exp2(s - m).astype(BF16), m))
        for (e, m), (_, _, vc, _, rows) in zip(probs, blocks):
            o = _dot(e, jnp.concatenate([vc, jnp.ones_like(vc)], axis=1))
            acc[p, rows, :] = jnp.where(head_a, o[:blk, :LANES], o[blk:, :LANES])
            mx[p, rows, :] = jnp.where(head_a, m[:blk], m[blk:])
            den[p, rows, :] = jnp.where(head_a, o[:blk, LANES:], o[blk:, LANES:])

    for p, (window, dil) in enumerate(DILATED_PATTERNS):
        assert window // dil == blk
        length = seq // dil
        nb = length // blk

        if dil == 1:
            for n0 in range(0, nb, DIL_GROUP):
                blocks = []
                for n in range(n0, min(n0 + DIL_GROUP, nb)):
                    keys = slice(max(n - 1, 0) * blk, (n + 1) * blk)
                    blocks.append((q_ref[0, n * blk:(n + 1) * blk, :], k_ref[0, keys, :], v_ref[0, keys, :],
                                   n == 0, pl.ds(n * blk, blk)))
                attend(p, blocks)
            continue

        per_step = max(DIL_GROUP // nb, 1)

        def residues(step, carry, p=p, dil=dil, length=length, nb=nb, per_step=per_step):
            blocks = []
            for j in range(per_step):
                r = step * per_step + j
                qr = qf[pl.ds(r, length, stride=dil), :].astype(BF16)
                kr = kf[pl.ds(r, length, stride=dil), :].astype(BF16)
                vr = vf[pl.ds(r, length, stride=dil), :].astype(BF16)
                for n in range(nb):
                    keys = slice(max(n - 1, 0) * blk, (n + 1) * blk)
                    blocks.append((qr[n * blk:(n + 1) * blk], kr[keys], vr[keys], n == 0,
                                   pl.ds(r + n * blk * dil, blk, stride=dil)))
            attend(p, blocks)
            return carry

        lax.fori_loop(0, dil // per_step, residues, 0)

    m_all = jnp.maximum(jnp.maximum(mx[0], mx[1]), mx[2])
    num = jnp.zeros((seq, LANES), F32)
    tot = jnp.zeros((seq, LANES), F32)
    for p in range(len(DILATED_PATTERNS)):
        w = jnp.exp2(mx[p] - m_all)
        num = num + w * acc[p]
        tot = tot + w * den[p]
    o_ref[0] = (num / tot).astype(BF16)


def _dilated_attention(proj):
    b, seq, _ = proj.shape
    n_pairs = A_WIDTH // LANES
    spec = lambda off: pl.BlockSpec((1, seq, LANES), lambda bi, pi: (bi, 0, off + pi))
    n_pat = len(DILATED_PATTERNS)
    return pl.pallas_call(
        _dilated_kernel,
        grid=(b, n_pairs),
        in_specs=[spec(0), spec(n_pairs), spec(2 * n_pairs)],
        out_specs=pl.BlockSpec((1, seq, LANES), lambda bi, pi: (bi, 0, pi)),
        out_shape=jax.ShapeDtypeStruct((b, seq, A_WIDTH), BF16),
        scratch_shapes=[pltpu.VMEM((seq, LANES), F32)] * 3
        + [pltpu.VMEM((n_pat, seq, LANES), F32)] * 3,
        compiler_params=_params(("parallel", "parallel")),
        name="dilated_attention",
    )(proj, proj, proj)


def _diff_kernel(lam_ref, q_ref, k_ref, v_ref, sg_ref, o_ref, vt_ref, *, lambda_init):
    lp = lam_ref[...]
    l1 = jnp.sum(lp[0:1] * lp[1:2], axis=-1, keepdims=True)
    l2 = jnp.sum(lp[2:3] * lp[3:4], axis=-1, keepdims=True)
    lam = jnp.exp(l1) - jnp.exp(l2) + lambda_init
    seq = q_ref.shape[1]
    tq = min(DIFF_Q_TILE, seq)
    lane = lax.broadcasted_iota(jnp.int32, (tq, LANES), 1)
    sub0 = lane < HEAD_DIM
    zero = jnp.zeros((tq, LANES), BF16)
    gain = sg_ref[...] * (1.0 - lambda_init)

    vt_ref[0:LANES, :] = v_ref[0].astype(F32).T.astype(BF16)
    pad_rows = vt_ref.shape[0] - LANES
    vt_ref[LANES:, :] = jnp.where(lax.broadcasted_iota(jnp.int32, (pad_rows, seq), 0) == 0, 1.0, 0.0).astype(BF16)

    causal = (lax.broadcasted_iota(jnp.int32, (tq, 2 * tq), 0)
              <= lax.broadcasted_iota(jnp.int32, (tq, 2 * tq), 1) % tq)

    def scores(i):
        q = q_ref[0, i * tq:(i + 1) * tq, :]
        q2 = jnp.concatenate([jnp.where(sub0, q, zero), jnp.where(sub0, zero, q)], axis=0)
        past = i * tq
        s_diag = jnp.where(causal, _dot_nt(k_ref[0, past:past + tq, :], q2), NEG_INF)
        s_past = _dot_nt(k_ref[0, 0:past, :], q2) if past else None
        return s_diag, s_past

    n_tiles = seq // tq
    ahead = [scores(i) for i in range(min(DIFF_LOOKAHEAD, n_tiles))]
    for i in range(n_tiles):
        s_diag, s_past = ahead.pop(0)
        if i + DIFF_LOOKAHEAD < n_tiles:
            ahead.append(scores(i + DIFF_LOOKAHEAD))
        past = i * tq
        m = jnp.max(s_diag, axis=0, keepdims=True)
        if past:
            m = jnp.maximum(m, jnp.max(s_past, axis=0, keepdims=True))
        o = _dot(vt_ref[:, past:past + tq], jnp.exp2(s_diag - m).astype(BF16))
        if past:
            o = o + _dot(vt_ref[:, 0:past], jnp.exp2(s_past - m).astype(BF16))
        o = o[:LANES] / o[LANES:LANES + 1]
        w = o[:, :tq] - lam * o[:, tq:]
        ms = jnp.mean(w * w, axis=0, keepdims=True)
        y = w * lax.rsqrt(ms + NORM_EPS) * gain
        o_ref[0, i * tq:(i + 1) * tq, :] = y.T.astype(BF16)


def _diff_attention(proj, lam_params, subln, lambda_init):
    b, seq, _ = proj.shape
    q_off = 3 * A_WIDTH // LANES
    k_off = q_off + DIFF_QK_WIDTH // LANES
    v_off = k_off + DIFF_QK_WIDTH // LANES
    spec = lambda off: pl.BlockSpec((1, seq, LANES), lambda bi, hi: (bi, 0, off + hi))
    return pl.pallas_call(
        functools.partial(_diff_kernel, lambda_init=lambda_init),
        grid=(b, DIFF_HEADS),
        in_specs=[_resident(lam_params.shape), spec(q_off), spec(k_off), spec(v_off),
                  _resident(subln.shape)],
        out_specs=pl.BlockSpec((1, seq, LANES), lambda bi, hi: (bi, 0, hi)),
        out_shape=jax.ShapeDtypeStruct((b, seq, DIFF_V_WIDTH), BF16),
        scratch_shapes=[pltpu.VMEM((DIFF_V_DIM + BF16_ROWS, seq), BF16)],
        compiler_params=_params(("parallel", "parallel")),
        name="diff_attention",
    )(lam_params, proj, proj, proj, subln)


def _rec_in_kernel(x_ref, g_ref, w_ref, lb_ref, o_ref, lf_ref, *, layer):
    h = _rms(x_ref[...], g_ref[...]).astype(BF16)
    lb_raw = lb_ref[...]
    lb_e = jnp.exp(lb_raw - jnp.max(lb_raw, axis=0, keepdims=True))
    lb_p = lb_e / jnp.sum(lb_e, axis=0, keepdims=True)
    lower = jnp.sum(lb_p[0:layer + 1], axis=0, keepdims=True) - lb_p[0:1]
    width = lf_ref.shape[1]
    n_cols = w_ref.shape[1]
    q_scale = HGRN_KEY_DIM ** -0.5
    starts = sorted(range(0, n_cols, COL_CHUNK), key=lambda c: (c // width == 2, c))
    for c0 in starts:
        y = _dot(h, w_ref[:, c0:c0 + COL_CHUNK].astype(BF16))
        group = c0 // width
        if group == 0:
            o_ref[:, c0:c0 + COL_CHUNK] = (_silu(y) * q_scale).astype(BF16)
        elif group == 1:
            f0 = c0 - width
            lo = lower[:, f0:f0 + COL_CHUNK]
            forget = lo + (1.0 - lo) * _sigmoid(y)
            lf_ref[:, f0:f0 + COL_CHUNK] = jnp.log2(forget)
            o_ref[:, c0:c0 + COL_CHUNK] = (1.0 - forget).astype(BF16)
        elif group == 2:
            o_ref[:, c0:c0 + COL_CHUNK] = y.astype(BF16)
        else:
            o_ref[:, c0:c0 + COL_CHUNK] = _silu(y).astype(BF16)


def _rec_in_proj(x2d, gain, w_stack, lower_bounds, layer, seq):
    t, d = x2d.shape
    n = w_stack.shape[2]
    width = lower_bounds.shape[1]
    tm = min(IN_ROW_TILE, seq)
    return pl.pallas_call(
        functools.partial(_rec_in_kernel, layer=layer),
        grid=(t // tm,),
        in_specs=[pl.BlockSpec((tm, d), lambda i: (i, 0)),
                  _resident((1, d)),
                  _layer_block(w_stack, layer),
                  _resident(lower_bounds.shape)],
        out_specs=[pl.BlockSpec((tm, n), lambda i: (i, 0)),
                   pl.BlockSpec((tm, width), lambda i: (i, 0))],
        out_shape=[jax.ShapeDtypeStruct((t, n), BF16),
                   jax.ShapeDtypeStruct((t, width), F32)],
        compiler_params=_params(("parallel",)),
        name="rec_in_proj",
    )(x2d, gain, w_stack, lower_bounds)


def _hgrn_levels():
    return [HGRN_CHUNK >> (i + 1) for i in range(HGRN_CHUNK.bit_length() - 1)]


def _hgrn_tables():
    ch = HGRN_CHUNK
    pair = 2 * ch
    t = np.arange(pair)
    col = np.arange(pair)[None, :]
    same_chunk = (col // ch) == (t[:, None] // ch)

    def prefix(idx):
        return (same_chunk & (col <= idx[:, None])).astype(np.float32)

    last = (t // ch) * ch + ch - 1
    blocks = [prefix(t), prefix(last) - prefix(t)]
    levels = _hgrn_levels()
    for h in levels:
        if h == 1:
            continue
        ref = (t & -(2 * h)) + h - 1
        sign = np.where((t & h) != 0, 1.0, -1.0)[:, None]
        blocks.append(sign * (prefix(t) - prefix(ref)))
    op = np.concatenate(blocks, axis=0)
    op = np.concatenate([op, op], axis=1)

    ti = np.arange(ch)[:, None]
    sj = np.arange(pair)[None, :] & (ch - 1)
    masks = [(ti == sj)]
    for h in levels:
        masks.append((((ti ^ sj) & -(2 * h)) == 0) & ((ti & h) != 0) & ((sj & h) == 0))
    masks.append((((ti ^ sj) & -(ch // 2)) == 0) & (sj <= ti))
    return jnp.asarray(op, BF16), jnp.asarray(np.stack(masks), F32)


def _hgrn_kernel(op_ref, mask_ref, q_ref, k_ref, v_ref, gate_ref, lf_ref, gn_ref, o_ref, st_ref):
    seq = q_ref.shape[1]
    ch = HGRN_CHUNK
    pair = 2 * ch
    st_ref[...] = jnp.zeros_like(st_ref)
    gain = gn_ref[...]
    halves = _hgrn_levels()
    first_chunk = lax.broadcasted_iota(jnp.int32, (ch, pair), 1) < ch
    row_id = lax.broadcasted_iota(jnp.int32, (pair, LANES), 0)
    odd_row = (row_id & 1) == 1
    first_half = (row_id & (ch // 2)) == 0
    zeros_half = jnp.zeros((ch, LANES), BF16)

    def side_by_side(x):
        return jnp.concatenate([x[:ch], x[ch:]], axis=1)

    def block_diag(x):
        return jnp.concatenate([jnp.concatenate([x[:ch], zeros_half], axis=1),
                                jnp.concatenate([zeros_half, x[ch:]], axis=1)], axis=0)

    group = min(HGRN_GROUP, seq // pair)

    def pair_rows(gi, c):
        return pl.ds(pl.multiple_of(gi * (group * pair), pair) + c * pair, pair)

    def chunk_bcast(b, row):
        return jnp.concatenate([jnp.broadcast_to(b[c0 + row:c0 + row + 1, :], (ch, LANES))
                                for c0 in (0, ch)], axis=0)

    def scores_any_decay(qb, kb, cum, g):
        attn = _dot_nt(side_by_side(qb), block_diag(kb)) * mask_ref[0]
        for li, h in enumerate(halves):
            if h == 1:
                e = jnp.where(odd_row, jnp.exp2(g), 1.0).astype(BF16)
            else:
                e = jnp.exp2(cum[(2 + li) * pair:(3 + li) * pair]).astype(BF16)
            attn = attn + _dot_nt(side_by_side(qb * e), block_diag(kb * e)) * mask_ref[1 + li]
        b = cum[:pair]
        return attn, qb * jnp.exp2(b).astype(BF16), kb * jnp.exp2(cum[pair:2 * pair]).astype(BF16)

    def scores_bounded_decay(qb, kb, b):
        mid = ch // 2
        d = b - chunk_bcast(b, mid - 1)
        e = jnp.exp2(-jnp.abs(d)).astype(BF16)
        q_top = qb * e
        attn = _dot_nt(side_by_side(q_top), block_diag(kb * e)) * mask_ref[1]
        q_dec = qb * jnp.exp2(b).astype(BF16)
        q_loc = jnp.where(first_half, q_dec, q_top)
        k_loc = kb * jnp.exp2(jnp.where(first_half, -b, -d)).astype(BF16)
        attn = attn + _dot_nt(side_by_side(q_loc), block_diag(k_loc)) * mask_ref[len(halves) + 1]
        k_dec = kb * jnp.exp2(chunk_bcast(b, ch - 1) - b).astype(BF16)
        return attn, q_dec, k_dec

    def decay_exponents(rows, bounded):
        op = op_ref[:pair, :] if bounded else op_ref[...]
        g = lf_ref[0, rows, :]
        g_hi = g.astype(BF16)
        g_lo = (g - g_hi.astype(F32)).astype(BF16)
        return g, _dot(op, jnp.concatenate([g_hi, g_lo], axis=0))

    def pair_scores(rows, g, cum, bounded):
        qb = q_ref[0, rows, :]
        kb = k_ref[0, rows, :]
        b = cum[:pair]
        if bounded:
            attn, q_dec, k_dec = scores_bounded_decay(qb, kb, b)
        else:
            attn, q_dec, k_dec = scores_any_decay(qb, kb, cum, g)
        cross = _dot_nt(q_dec[ch:], k_dec)
        lhs = jnp.concatenate([jnp.where(first_chunk, attn, 0.0),
                               jnp.where(first_chunk, cross, attn)], axis=0).astype(BF16)
        tot0 = jnp.exp2(b[ch - 1:ch, :])
        tot1 = jnp.exp2(b[pair - 1:pair, :])
        q_in = jnp.concatenate([q_dec[:ch], q_dec[ch:] * tot0.astype(BF16)], axis=0)
        k_up = jnp.concatenate([k_dec[:ch] * tot1.astype(BF16), k_dec[ch:]], axis=0)
        return lhs, q_in, k_up, tot0 * tot1

    def step(gi, carry, bounded):
        ahead_exp, ahead_sc = HGRN_AHEAD
        exps, scs = {}, {}
        st = st_ref[...]
        for tick in range(group + ahead_exp + ahead_sc):
            if tick < group:
                exps[tick] = decay_exponents(pair_rows(gi, tick), bounded)
            c = tick - ahead_exp
            if 0 <= c < group:
                scs[c] = pair_scores(pair_rows(gi, c), *exps.pop(c), bounded)
            c = tick - ahead_exp - ahead_sc
            if 0 <= c < group:
                rows = pair_rows(gi, c)
                v = v_ref[0, rows, :]
                lhs, q_in, k_up, total = scs.pop(c)
                o = _dot_nt(q_in, st.astype(BF16)) + _dot(lhs, v)
                st = st * total + _dot_tn(v, k_up)
                y = _rms(o, gain) * gate_ref[0, rows, :].astype(F32)
                o_ref[0, rows, :] = y.astype(BF16)
        st_ref[...] = st
        return carry

    half_rows = ch // 2
    half_tot = jnp.sum(lf_ref[0].reshape(seq // half_rows, half_rows, LANES), axis=1)
    worst = jnp.min(jnp.min(half_tot, axis=0, keepdims=True), axis=1, keepdims=True)
    bounded_ok = worst[0, 0] >= HGRN_MIN_HALF_LOG2
    n_steps = seq // (group * pair)

    @pl.when(bounded_ok)
    def _():
        lax.fori_loop(0, n_steps, functools.partial(step, bounded=True), 0)

    @pl.when(jnp.logical_not(bounded_ok))
    def _():
        lax.fori_loop(0, n_steps, functools.partial(step, bounded=False), 0)


def _hgrn_recurrence(proj, logf, gnorm):
    b, seq, _ = proj.shape
    h = HGRN_HEADS
    spec = lambda off: pl.BlockSpec((1, seq, LANES), lambda bi, hi: (bi, 0, off + hi))
    op, masks = _hgrn_tables()
    return pl.pallas_call(
        _hgrn_kernel,
        grid=(b, h),
        in_specs=[_resident(op.shape), _resident(masks.shape),
                  spec(0), spec(h), spec(2 * h), spec(3 * h), spec(0), _resident(gnorm.shape)],
        out_specs=pl.BlockSpec((1, seq, LANES), lambda bi, hi: (bi, 0, hi)),
        out_shape=jax.ShapeDtypeStruct((b, seq, h * HGRN_VAL_DIM), BF16),
        scratch_shapes=[pltpu.VMEM((HGRN_VAL_DIM, HGRN_KEY_DIM), F32)],
        compiler_params=_params(("parallel", "parallel")),
        name="hgrn_recurrence",
    )(op, masks, proj, proj, proj, proj, logf, gnorm)


def _post_kernel(*refs, n_mix, hidden, chunks):
    mix_refs = refs[:n_mix]
    wo_ref, x_ref, g_ref, win_ref, wout_ref, o_ref = refs[n_mix:]
    gains = g_ref[...]
    mixed = jnp.concatenate([r[...] for r in mix_refs], axis=-1) if n_mix > 1 else mix_refs[0][...]
    x1 = x_ref[...] + _rms(_dot(mixed, wo_ref[...].astype(BF16)), gains[1:2])
    h = _rms(x1, gains[2:3]).astype(BF16)
    acc = jnp.zeros(x1.shape, F32)
    for lo, hi in chunks:
        gate = _dot(h, win_ref[:, lo:hi].astype(BF16))
        up = _dot(h, win_ref[:, hidden + lo:hidden + hi].astype(BF16))
        act = (_silu(gate) * up).astype(BF16)
        acc = acc + _dot(act, wout_ref[lo:hi, :].astype(BF16))
    o_ref[...] = x1 + _rms(acc, gains[3:4])


def _post_mixer(mix_list, w_out, mixer_layer, x2d, gains, ffn_w_in, ffn_w_out, layer, seq):
    t, d = x2d.shape
    hidden = ffn_w_out.shape[1]
    tm = min(ROW_TILE, seq)
    chunks = tuple((lo, min(lo + COL_CHUNK, hidden)) for lo in range(0, hidden, COL_CHUNK))
    mix_specs = [pl.BlockSpec((tm, m.shape[1]), lambda i: (i, 0)) for m in mix_list]
    return pl.pallas_call(
        functools.partial(_post_kernel, n_mix=len(mix_list), hidden=hidden, chunks=chunks),
        grid=(t // tm,),
        in_specs=mix_specs + [_layer_block(w_out, mixer_layer),
                              pl.BlockSpec((tm, d), lambda i: (i, 0)),
                              _layer_block(gains, layer),
                              _layer_block(ffn_w_in, layer),
                              _layer_block(ffn_w_out, layer)],
        out_specs=pl.BlockSpec((tm, d), lambda i: (i, 0)),
        out_shape=jax.ShapeDtypeStruct((t, d), F32),
        compiler_params=_params(("parallel",)),
        name="post_mixer_ffn",
    )(*mix_list, w_out, x2d, gains, ffn_w_in, ffn_w_out)


def _rope_tables(seq):
    half = HEAD_DIM // 2
    inv_freq = ROPE_THETA ** (-jnp.arange(0, HEAD_DIM, 2, dtype=F32) / HEAD_DIM)
    ang = jnp.arange(seq, dtype=F32)[:, None] * inv_freq[None, :]
    reps = LANES // half
    sign = jnp.tile(jnp.concatenate([-jnp.ones((half,), F32), jnp.ones((half,), F32)]), LANES // HEAD_DIM)
    return jnp.tile(jnp.cos(ang), (1, reps)), jnp.tile(jnp.sin(ang), (1, reps)) * sign[None, :]


def kernel(x, norm_gains, att_w_in, att_lambda, att_subln, att_w_out, rec_w_in, rec_lower_bounds,
           rec_gnorm, rec_w_out, ffn_w_in, ffn_w_out):
    b, seq, d = x.shape
    depth = norm_gains.shape[0]
    cos, sin = _rope_tables(seq)
    x2d = x.reshape(b * seq, d)
    for layer in range(depth):
        gain_in = norm_gains[layer, 0:1]
        j = layer // 2
        if layer % 2 == 0:
            lambda_init = 0.8 - 0.6 * math.exp(-0.3 * layer)
            proj = _att_in_proj(x2d, gain_in, att_w_in, j, cos, sin, seq)
            proj = proj.reshape(b, seq, -1)
            oa = _dilated_attention(proj)
            od = _diff_attention(proj, att_lambda[j], att_subln[j][:, None], lambda_init)
            mix = [oa.reshape(b * seq, -1), od.reshape(b * seq, -1)]
            w_out = att_w_out
        else:
            proj, logf = _rec_in_proj(x2d, gain_in, rec_w_in, rec_lower_bounds, j, seq)
            o = _hgrn_recurrence(proj.reshape(b, seq, -1), logf.reshape(b, seq, -1), rec_gnorm[j][None, :])
            mix = [o.reshape(b * seq, -1)]
            w_out = rec_w_out
        x2d = _post_mixer(mix, w_out, j, x2d, norm_gains, ffn_w_in, ffn_w_out, layer, seq)
    return x2d.reshape(b, seq, d)
```

```python
import functools
import math

import jax
import jax.numpy as jnp
import numpy as np
from jax import lax
from jax.experimental import pallas as pl
from jax.experimental.pallas import tpu as pltpu

F32 = jnp.float32
BF16 = jnp.bfloat16

HEAD_DIM = 64
A_HEADS = 8
A_WIDTH = A_HEADS * HEAD_DIM
DILATED_PATTERNS = ((128, 1), (512, 4), (2048, 16))
DIL_BLOCK = 128
DIFF_HEADS = 4
DIFF_QK_WIDTH = 2 * DIFF_HEADS * HEAD_DIM
DIFF_V_DIM = 2 * HEAD_DIM
DIFF_V_WIDTH = DIFF_HEADS * DIFF_V_DIM
ROPE_THETA = 10000.0
HGRN_HEADS = 8
HGRN_KEY_DIM = 128
HGRN_VAL_DIM = 128
NORM_EPS = 1e-6
NEG_INF = -1e30
LOG2_E = math.log2(math.e)

LANES = 128
BF16_ROWS = 16
V7X_VMEM_BYTES = 64 * 1024 * 1024
VMEM_LIMIT = V7X_VMEM_BYTES * 7 // 8

ROW_TILE = 512
IN_ROW_TILE = 1024
COL_CHUNK = 512
DIL_GROUP = 8
DIFF_Q_TILE = 256
DIFF_LOOKAHEAD = 2
HGRN_CHUNK = 64
HGRN_GROUP = 16
HGRN_AHEAD = (2, 2)
HGRN_MIN_HALF_LOG2 = -96.0


def _rms(x, gain):
    ms = jnp.mean(x * x, axis=-1, keepdims=True)
    return x * lax.rsqrt(ms + NORM_EPS) * gain


def _sigmoid(x):
    return 1.0 / (1.0 + jnp.exp(-x))


def _silu(x):
    return x * (0.5 * jnp.tanh(0.5 * x) + 0.5)


def _dot(a, b):
    return jnp.dot(a, b, preferred_element_type=F32)


def _dot_nt(a, b):
    return lax.dot_general(a, b, (((1,), (1,)), ((), ())), preferred_element_type=F32)


def _dot_tn(a, b):
    return lax.dot_general(a, b, (((0,), (0,)), ((), ())), preferred_element_type=F32)


def _params(semantics):
    return pltpu.CompilerParams(dimension_semantics=semantics, vmem_limit_bytes=VMEM_LIMIT)


def _resident(shape):
    return pl.BlockSpec(shape, lambda *_: (0,) * len(shape), pipeline_mode=pl.Buffered(1))


def _layer_block(stack, layer):
    tail = stack.shape[1:]
    return pl.BlockSpec((None,) + tail, lambda *_: (layer,) + (0,) * len(tail), pipeline_mode=pl.Buffered(1))


def _att_in_kernel(x_ref, g_ref, w_ref, cos_ref, sin_ref, o_ref):
    h = _rms(x_ref[...], g_ref[...]).astype(BF16)
    tm = h.shape[0]
    cos = cos_ref[...]
    sin = sin_ref[...]
    lane = lax.broadcasted_iota(jnp.int32, (tm, LANES), 1)
    first_half = (lane % HEAD_DIM) < (HEAD_DIM // 2)
    n_cols = w_ref.shape[1]
    q_scale = HEAD_DIM ** -0.5 * LOG2_E
    rope_hi = 2 * A_WIDTH
    diff_lo = 3 * A_WIDTH
    diff_rope_hi = diff_lo + 2 * DIFF_QK_WIDTH
    for c0 in range(0, n_cols, COL_CHUNK):
        acc = _dot(h, w_ref[:, c0:c0 + COL_CHUNK].astype(BF16))
        for l0 in range(0, COL_CHUNK, LANES):
            col = c0 + l0
            y = acc[:, l0:l0 + LANES]
            if col < rope_hi or diff_lo <= col < diff_rope_hi:
                rot = jnp.where(first_half, pltpu.roll(y, LANES - HEAD_DIM // 2, 1),
                                pltpu.roll(y, HEAD_DIM // 2, 1))
                y = y * cos + rot * sin
                if col < A_WIDTH or diff_lo <= col < diff_lo + DIFF_QK_WIDTH:
                    y = y * q_scale
            o_ref[:, col:col + LANES] = y.astype(BF16)


def _att_in_proj(x2d, gain, w_stack, layer, cos, sin, seq):
    t, d = x2d.shape
    n = w_stack.shape[2]
    tm = min(IN_ROW_TILE, seq)
    tiles_per_seq = seq // tm
    return pl.pallas_call(
        _att_in_kernel,
        grid=(t // tm,),
        in_specs=[pl.BlockSpec((tm, d), lambda i: (i, 0)),
                  _resident((1, d)),
                  _layer_block(w_stack, layer),
                  pl.BlockSpec((tm, LANES), lambda i: (i % tiles_per_seq, 0)),
                  pl.BlockSpec((tm, LANES), lambda i: (i % tiles_per_seq, 0))],
        out_specs=pl.BlockSpec((tm, n), lambda i: (i, 0)),
        out_shape=jax.ShapeDtypeStruct((t, n), BF16),
        compiler_params=_params(("parallel",)),
        name="att_in_proj",
    )(x2d, gain, w_stack, cos, sin)


def _dilated_kernel(q_ref, k_ref, v_ref, o_ref, qf, kf, vf, acc, mx, den):
    seq = q_ref.shape[1]
    blk = DIL_BLOCK
    qf[...] = q_ref[0].astype(F32)
    kf[...] = k_ref[0].astype(F32)
    vf[...] = v_ref[0].astype(F32)

    lane = lax.broadcasted_iota(jnp.int32, (blk, LANES), 1)
    head_a = lane < HEAD_DIM
    qi = lax.broadcasted_iota(jnp.int32, (2 * blk, 2 * blk), 0) % blk
    ci = lax.broadcasted_iota(jnp.int32, (2 * blk, 2 * blk), 1)
    mask_two = (ci >= qi) & (ci <= qi + blk)
    mask_first = (lax.broadcasted_iota(jnp.int32, (blk, 2 * blk), 1) % blk
                  <= lax.broadcasted_iota(jnp.int32, (blk, 2 * blk), 0))

    zero = jnp.zeros((blk, LANES), BF16)
    ones_a = jnp.where(head_a, 1.0, 0.0).astype(BF16)
    ones_b = jnp.where(head_a, 0.0, 1.0).astype(BF16)

    def attend(p, blocks):
        scores = []
        for qb, kc, _, first, _ in blocks:
            if first:
                scores.append(_dot_nt(qb, jnp.concatenate([jnp.where(head_a, kc, zero),
                                                           jnp.where(head_a, zero, kc)], axis=0)))
            else:
                q2 = jnp.concatenate([jnp.where(head_a, qb, zero), jnp.where(head_a, zero, qb)], axis=0)
                scores.append(_dot_nt(q2, kc))
        probs = []
        for s, (_, _, _, first, _) in zip(scores, blocks):
            if first:
                s = jnp.where(mask_first, s, NEG_INF)
                m_a = jnp.max(s[:, :blk], axis=-1, keepdims=True)
                m_b = jnp.max(s[:, blk:], axis=-1, keepdims=True)
                e = jnp.concatenate([jnp.exp2(s[:, :blk] - m_a), jnp.exp2(s[:, blk:] - m_b)], axis=1)
                probs.append((e.astype(BF16), jnp.where(head_a, m_a, m_b)))
            else:
                s = jnp.where(mask_two, s, NEG_INF)
                m = jnp.max(s, axis=-1, keepdims=True)
                probs.append((jnp.exp2(s - m).astype(BF16), m))
        for (e, m), (_, _, vc, first, rows) in zip(probs, blocks):
            if first:
                v_a = jnp.concatenate([jnp.where(head_a, vc, zero), ones_a], axis=1)
                v_b = jnp.concatenate([jnp.where(head_a, zero, vc), ones_b], axis=1)
                o = _dot(e, jnp.concatenate([v_a, v_b], axis=0))
                acc[p, rows, :] = o[:, :LANES]
                mx[p, rows, :] = m
                den[p, rows, :] = o[:, LANES:]
            else:
                o = _dot(e, jnp.concatenate([vc, jnp.ones_like(vc)], axis=1))
                acc[p, rows, :] = jnp.where(head_a, o[:blk, :LANES], o[blk:, :LANES])
                mx[p, rows, :] = jnp.where(head_a, m[:blk], m[blk:])
                den[p, rows, :] = jnp.where(head_a, o[:blk, LANES:], o[blk:, LANES:])

    for p, (window, dil) in enumerate(DILATED_PATTERNS):
        assert window // dil == blk
        length = seq // dil
        nb = length // blk

        if dil == 1:
            for n0 in range(0, nb, DIL_GROUP):
                blocks = []
                for n in range(n0, min(n0 + DIL_GROUP, nb)):
                    keys = slice(max(n - 1, 0) * blk, (n + 1) * blk)
                    blocks.append((q_ref[0, n * blk:(n + 1) * blk, :], k_ref[0, keys, :], v_ref[0, keys, :],
                                   n == 0, pl.ds(n * blk, blk)))
                attend(p, blocks)
            continue

        per_step = max(DIL_GROUP // nb, 1)

        def residues(step, carry, p=p, dil=dil, length=length, nb=nb, per_step=per_step):
            blocks = []
            for j in range(per_step):
                r = step * per_step + j
                qr = qf[pl.ds(r, length, stride=dil), :].astype(BF16)
                kr = kf[pl.ds(r, length, stride=dil), :].astype(BF16)
                vr = vf[pl.ds(r, length, stride=dil), :].astype(BF16)
                for n in range(nb):
                    keys = slice(max(n - 1, 0) * blk, (n + 1) * blk)
                    blocks.append((qr[n * blk:(n + 1) * blk], kr[keys], vr[keys], n == 0,
                                   pl.ds(r + n * blk * dil, blk, stride=dil)))
            attend(p, blocks)
            return carry

        lax.fori_loop(0, dil // per_step, residues, 0)

    m_all = jnp.maximum(jnp.maximum(mx[0], mx[1]), mx[2])
    num = jnp.zeros((seq, LANES), F32)
    tot = jnp.zeros((seq, LANES), F32)
    for p in range(len(DILATED_PATTERNS)):
        w = jnp.exp2(mx[p] - m_all)
        num = num + w * acc[p]
        tot = tot + w * den[p]
    o_ref[0] = (num / tot).astype(BF16)


def _dilated_attention(proj):
    b, seq, _ = proj.shape
    n_pairs = A_WIDTH // LANES
    spec = lambda off: pl.BlockSpec((1, seq, LANES), lambda bi, pi: (bi, 0, off + pi))
    n_pat = len(DILATED_PATTERNS)
    return pl.pallas_call(
        _dilated_kernel,
        grid=(b, n_pairs),
        in_specs=[spec(0), spec(n_pairs), spec(2 * n_pairs)],
        out_specs=pl.BlockSpec((1, seq, LANES), lambda bi, pi: (bi, 0, pi)),
        out_shape=jax.ShapeDtypeStruct((b, seq, A_WIDTH), BF16),
        scratch_shapes=[pltpu.VMEM((seq, LANES), F32)] * 3
        + [pltpu.VMEM((n_pat, seq, LANES), F32)] * 3,
        compiler_params=_params(("parallel", "parallel")),
        name="dilated_attention",
    )(proj, proj, proj)


def _diff_kernel(lam_ref, q_ref, k_ref, v_ref, sg_ref, o_ref, vt_ref, *, lambda_init):
    lp = lam_ref[...]
    l1 = jnp.sum(lp[0:1] * lp[1:2], axis=-1, keepdims=True)
    l2 = jnp.sum(lp[2:3] * lp[3:4], axis=-1, keepdims=True)
    lam = jnp.exp(l1) - jnp.exp(l2) + lambda_init
    seq = q_ref.shape[1]
    tq = min(DIFF_Q_TILE, seq)
    lane = lax.broadcasted_iota(jnp.int32, (tq, LANES), 1)
    sub0 = lane < HEAD_DIM
    zero = jnp.zeros((tq, LANES), BF16)
    gain = sg_ref[...] * (1.0 - lambda_init)

    vt_ref[0:LANES, :] = v_ref[0].astype(F32).T.astype(BF16)
    pad_rows = vt_ref.shape[0] - LANES
    vt_ref[LANES:, :] = jnp.where(lax.broadcasted_iota(jnp.int32, (pad_rows, seq), 0) == 0, 1.0, 0.0).astype(BF16)

    causal = (lax.broadcasted_iota(jnp.int32, (tq, 2 * tq), 0)
              <= lax.broadcasted_iota(jnp.int32, (tq, 2 * tq), 1) % tq)

    def scores(i):
        q = q_ref[0, i * tq:(i + 1) * tq, :]
        q2 = jnp.concatenate([jnp.where(sub0, q, zero), jnp.where(sub0, zero, q)], axis=0)
        past = i * tq
        s_diag = jnp.where(causal, _dot_nt(k_ref[0, past:past + tq, :], q2), NEG_INF)
        s_past = _dot_nt(k_ref[0, 0:past, :], q2) if past else None
        return s_diag, s_past

    n_tiles = seq // tq
    ahead = [scores(i) for i in range(min(DIFF_LOOKAHEAD, n_tiles))]
    for i in range(n_tiles):
        s_diag, s_past = ahead.pop(0)
        if i + DIFF_LOOKAHEAD < n_tiles:
            ahead.append(scores(i + DIFF_LOOKAHEAD))
        past = i * tq
        m = jnp.max(s_diag, axis=0, keepdims=True)
        if past:
            m = jnp.maximum(m, jnp.max(s_past, axis=0, keepdims=True))
        o = _dot(vt_ref[:, past:past + tq], jnp.exp2(s_diag - m).astype(BF16))
        if past:
            o = o + _dot(vt_ref[:, 0:past], jnp.exp2(s_past - m).astype(BF16))
        o = o[:LANES] / o[LANES:LANES + 1]
        w = o[:, :tq] - lam * o[:, tq:]
        ms = jnp.mean(w * w, axis=0, keepdims=True)
        y = w * lax.rsqrt(ms + NORM_EPS) * gain
        o_ref[0, i * tq:(i + 1) * tq, :] = y.T.astype(BF16)


def _diff_attention(proj, lam_params, subln, lambda_init):
    b, seq, _ = proj.shape
    q_off = 3 * A_WIDTH // LANES
    k_off = q_off + DIFF_QK_WIDTH // LANES
    v_off = k_off + DIFF_QK_WIDTH // LANES
    spec = lambda off: pl.BlockSpec((1, seq, LANES), lambda bi, hi: (bi, 0, off + hi))
    return pl.pallas_call(
        functools.partial(_diff_kernel, lambda_init=lambda_init),
        grid=(b, DIFF_HEADS),
        in_specs=[_resident(lam_params.shape), spec(q_off), spec(k_off), spec(v_off),
                  _resident(subln.shape)],
        out_specs=pl.BlockSpec((1, seq, LANES), lambda bi, hi: (bi, 0, hi)),
        out_shape=jax.ShapeDtypeStruct((b, seq, DIFF_V_WIDTH), BF16),
        scratch_shapes=[pltpu.VMEM((DIFF_V_DIM + BF16_ROWS, seq), BF16)],
        compiler_params=_params(("parallel", "parallel")),
        name="diff_attention",
    )(lam_params, proj, proj, proj, subln)


def _rec_in_kernel(x_ref, g_ref, w_ref, lb_ref, o_ref, lf_ref, *, layer):
    h = _rms(x_ref[...], g_ref[...]).astype(BF16)
    lb_raw = lb_ref[...]
    lb_e = jnp.exp(lb_raw - jnp.max(lb_raw, axis=0, keepdims=True))
    lb_p = lb_e / jnp.sum(lb_e, axis=0, keepdims=True)
    lower = jnp.sum(lb_p[0:layer + 1], axis=0, keepdims=True) - lb_p[0:1]
    width = lf_ref.shape[1]
    n_cols = w_ref.shape[1]
    q_scale = HGRN_KEY_DIM ** -0.5
    starts = sorted(range(0, n_cols, COL_CHUNK), key=lambda c: (c // width == 2, c))
    for c0 in starts:
        y = _dot(h, w_ref[:, c0:c0 + COL_CHUNK].astype(BF16))
        group = c0 // width
        if group == 0:
            o_ref[:, c0:c0 + COL_CHUNK] = (_silu(y) * q_scale).astype(BF16)
        elif group == 1:
            f0 = c0 - width
            lo = lower[:, f0:f0 + COL_CHUNK]
            forget = lo + (1.0 - lo) * _sigmoid(y)
            lf_ref[:, f0:f0 + COL_CHUNK] = jnp.log2(forget)
            o_ref[:, c0:c0 + COL_CHUNK] = (1.0 - forget).astype(BF16)
        elif group == 2:
            o_ref[:, c0:c0 + COL_CHUNK] = y.astype(BF16)
        else:
            o_ref[:, c0:c0 + COL_CHUNK] = _silu(y).astype(BF16)


def _rec_in_proj(x2d, gain, w_stack, lower_bounds, layer, seq):
    t, d = x2d.shape
    n = w_stack.shape[2]
    width = lower_bounds.shape[1]
    tm = min(IN_ROW_TILE, seq)
    return pl.pallas_call(
        functools.partial(_rec_in_kernel, layer=layer),
        grid=(t // tm,),
        in_specs=[pl.BlockSpec((tm, d), lambda i: (i, 0)),
                  _resident((1, d)),
                  _layer_block(w_stack, layer),
                  _resident(lower_bounds.shape)],
        out_specs=[pl.BlockSpec((tm, n), lambda i: (i, 0)),
                   pl.BlockSpec((tm, width), lambda i: (i, 0))],
        out_shape=[jax.ShapeDtypeStruct((t, n), BF16),
                   jax.ShapeDtypeStruct((t, width), F32)],
        compiler_params=_params(("parallel",)),
        name="rec_in_proj",
    )(x2d, gain, w_stack, lower_bounds)


def _hgrn_levels():
    return [HGRN_CHUNK >> (i + 1) for i in range(HGRN_CHUNK.bit_length() - 1)]


def _hgrn_tables():
    ch = HGRN_CHUNK
    pair = 2 * ch
    t = np.arange(pair)
    col = np.arange(pair)[None, :]
    same_chunk = (col // ch) == (t[:, None] // ch)

    def prefix(idx):
        return (same_chunk & (col <= idx[:, None])).astype(np.float32)

    last = (t // ch) * ch + ch - 1
    blocks = [prefix(t), prefix(last) - prefix(t)]
    levels = _hgrn_levels()
    for h in levels:
        if h == 1:
            continue
        ref = (t & -(2 * h)) + h - 1
        sign = np.where((t & h) != 0, 1.0, -1.0)[:, None]
        blocks.append(sign * (prefix(t) - prefix(ref)))
    op = np.concatenate(blocks, axis=0)
    op = np.concatenate([op, op], axis=1)

    ti = np.arange(ch)[:, None]
    sj = np.arange(pair)[None, :] & (ch - 1)
    masks = [(ti == sj)]
    for h in levels:
        masks.append((((ti ^ sj) & -(2 * h)) == 0) & ((ti & h) != 0) & ((sj & h) == 0))
    masks.append((((ti ^ sj) & -(ch // 2)) == 0) & (sj <= ti))
    return jnp.asarray(op, BF16), jnp.asarray(np.stack(masks), F32)


def _hgrn_kernel(op_ref, mask_ref, q_ref, k_ref, v_ref, gate_ref, lf_ref, gn_ref, o_ref, st_ref):
    seq = q_ref.shape[1]
    ch = HGRN_CHUNK
    pair = 2 * ch
    st_ref[...] = jnp.zeros_like(st_ref)
    gain = gn_ref[...]
    halves = _hgrn_levels()
    first_chunk = lax.broadcasted_iota(jnp.int32, (ch, pair), 1) < ch
    row_id = lax.broadcasted_iota(jnp.int32, (pair, LANES), 0)
    odd_row = (row_id & 1) == 1
    first_half = (row_id & (ch // 2)) == 0
    zeros_half = jnp.zeros((ch, LANES), BF16)

    def side_by_side(x):
        return jnp.concatenate([x[:ch], x[ch:]], axis=1)

    def block_diag(x):
        return jnp.concatenate([jnp.concatenate([x[:ch], zeros_half], axis=1),
                                jnp.concatenate([zeros_half, x[ch:]], axis=1)], axis=0)

    group = min(HGRN_GROUP, seq // pair)

    def pair_rows(gi, c):
        return pl.ds(pl.multiple_of(gi * (group * pair), pair) + c * pair, pair)

    def chunk_bcast(b, row):
        return jnp.concatenate([jnp.broadcast_to(b[c0 + row:c0 + row + 1, :], (ch, LANES))
                                for c0 in (0, ch)], axis=0)

    def scores_any_decay(qb, kb, cum, g):
        attn = _dot_nt(side_by_side(qb), block_diag(kb)) * mask_ref[0]
        for li, h in enumerate(halves):
            if h == 1:
                e = jnp.where(odd_row, jnp.exp2(g), 1.0).astype(BF16)
            else:
                e = jnp.exp2(cum[(2 + li) * pair:(3 + li) * pair]).astype(BF16)
            attn = attn + _dot_nt(side_by_side(qb * e), block_diag(kb * e)) * mask_ref[1 + li]
        b = cum[:pair]
        return attn, qb * jnp.exp2(b).astype(BF16), kb * jnp.exp2(cum[pair:2 * pair]).astype(BF16)

    def scores_bounded_decay(qb, kb, b):
        mid = ch // 2
        d = b - chunk_bcast(b, mid - 1)
        e = jnp.exp2(-jnp.abs(d)).astype(BF16)
        q_top = qb * e
        attn = _dot_nt(side_by_side(q_top), block_diag(kb * e)) * mask_ref[1]
        q_dec = qb * jnp.exp2(b).astype(BF16)
        q_loc = jnp.where(first_half, q_dec, q_top)
        k_loc = kb * jnp.exp2(jnp.where(first_half, -b, -d)).astype(BF16)
        attn = attn + _dot_nt(side_by_side(q_loc), block_diag(k_loc)) * mask_ref[len(halves) + 1]
        k_dec = kb * jnp.exp2(chunk_bcast(b, ch - 1) - b).astype(BF16)
        return attn, q_dec, k_dec

    def decay_exponents(rows, bounded):
        op = op_ref[:pair, :] if bounded else op_ref[...]
        g = lf_ref[0, rows, :]
        g_hi = g.astype(BF16)
        g_lo = (g - g_hi.astype(F32)).astype(BF16)
        return g, _dot(op, jnp.concatenate([g_hi, g_lo], axis=0))

    def pair_scores(rows, g, cum, bounded):
        qb = q_ref[0, rows, :]
        kb = k_ref[0, rows, :]
        b = cum[:pair]
        if bounded:
            attn, q_dec, k_dec = scores_bounded_decay(qb, kb, b)
        else:
            attn, q_dec, k_dec = scores_any_decay(qb, kb, cum, g)
        cross = _dot_nt(q_dec[ch:], k_dec)
        lhs = jnp.concatenate([jnp.where(first_chunk, attn, 0.0),
                               jnp.where(first_chunk, cross, attn)], axis=0).astype(BF16)
        tot0 = jnp.exp2(b[ch - 1:ch, :])
        tot1 = jnp.exp2(b[pair - 1:pair, :])
        q_in = jnp.concatenate([q_dec[:ch], q_dec[ch:] * tot0.astype(BF16)], axis=0)
        k_up = jnp.concatenate([k_dec[:ch] * tot1.astype(BF16), k_dec[ch:]], axis=0)
        return lhs, q_in, k_up, tot0 * tot1

    def step(gi, carry, bounded):
        ahead_exp, ahead_sc = HGRN_AHEAD
        exps, scs = {}, {}
        st = st_ref[...]
        for tick in range(group + ahead_exp + ahead_sc):
            if tick < group:
                exps[tick] = decay_exponents(pair_rows(gi, tick), bounded)
            c = tick - ahead_exp
            if 0 <= c < group:
                scs[c] = pair_scores(pair_rows(gi, c), *exps.pop(c), bounded)
            c = tick - ahead_exp - ahead_sc
            if 0 <= c < group:
                rows = pair_rows(gi, c)
                v = v_ref[0, rows, :]
                lhs, q_in, k_up, total = scs.pop(c)
                o = _dot_nt(q_in, st.astype(BF16)) + _dot(lhs, v)
                st = st * total + _dot_tn(v, k_up)
                y = _rms(o, gain) * gate_ref[0, rows, :].astype(F32)
                o_ref[0, rows, :] = y.astype(BF16)
        st_ref[...] = st
        return carry

    half_rows = ch // 2
    half_tot = jnp.sum(lf_ref[0].reshape(seq // half_rows, half_rows, LANES), axis=1)
    worst = jnp.min(jnp.min(half_tot, axis=0, keepdims=True), axis=1, keepdims=True)
    bounded_ok = worst[0, 0] >= HGRN_MIN_HALF_LOG2
    n_steps = seq // (group * pair)

    @pl.when(bounded_ok)
    def _():
        lax.fori_loop(0, n_steps, functools.partial(step, bounded=True), 0)

    @pl.when(jnp.logical_not(bounded_ok))
    def _():
        lax.fori_loop(0, n_steps, functools.partial(step, bounded=False), 0)


def _hgrn_recurrence(proj, logf, gnorm):
    b, seq, _ = proj.shape
    h = HGRN_HEADS
    spec = lambda off: pl.BlockSpec((1, seq, LANES), lambda bi, hi: (bi, 0, off + hi))
    op, masks = _hgrn_tables()
    return pl.pallas_call(
        _hgrn_kernel,
        grid=(b, h),
        in_specs=[_resident(op.shape), _resident(masks.shape),
                  spec(0), spec(h), spec(2 * h), spec(3 * h), spec(0), _resident(gnorm.shape)],
        out_specs=pl.BlockSpec((1, seq, LANES), lambda bi, hi: (bi, 0, hi)),
        out_shape=jax.ShapeDtypeStruct((b, seq, h * HGRN_VAL_DIM), BF16),
        scratch_shapes=[pltpu.VMEM((HGRN_VAL_DIM, HGRN_KEY_DIM), F32)],
        compiler_params=_params(("parallel", "parallel")),
        name="hgrn_recurrence",
    )(op, masks, proj, proj, proj, proj, logf, gnorm)


def _post_kernel(*refs, n_mix, hidden, chunks):
    mix_refs = refs[:n_mix]
    wo_ref, x_ref, g_ref, win_ref, wout_ref, o_ref = refs[n_mix:]
    gains = g_ref[...]
    mixed = jnp.concatenate([r[...] for r in mix_refs], axis=-1) if n_mix > 1 else mix_refs[0][...]
    x1 = x_ref[...] + _rms(_dot(mixed, wo_ref[...].astype(BF16)), gains[1:2])
    h = _rms(x1, gains[2:3]).astype(BF16)
    acc = jnp.zeros(x1.shape, F32)
    for lo, hi in chunks:
        gate = _dot(h, win_ref[:, lo:hi].astype(BF16))
        up = _dot(h, win_ref[:, hidden + lo:hidden + hi].astype(BF16))
        act = (_silu(gate) * up).astype(BF16)
        acc = acc + _dot(act, wout_ref[lo:hi, :].astype(BF16))
    o_ref[...] = x1 + _rms(acc, gains[3:4])


def _post_mixer(mix_list, w_out, mixer_layer, x2d, gains, ffn_w_in, ffn_w_out, layer, seq):
    t, d = x2d.shape
    hidden = ffn_w_out.shape[1]
    tm = min(ROW_TILE, seq)
    chunks = tuple((lo, min(lo + COL_CHUNK, hidden)) for lo in range(0, hidden, COL_CHUNK))
    mix_specs = [pl.BlockSpec((tm, m.shape[1]), lambda i: (i, 0)) for m in mix_list]
    return pl.pallas_call(
        functools.partial(_post_kernel, n_mix=len(mix_list), hidden=hidden, chunks=chunks),
        grid=(t // tm,),
        in_specs=mix_specs + [_layer_block(w_out, mixer_layer),
                              pl.BlockSpec((tm, d), lambda i: (i, 0)),
                              _layer_block(gains, layer),
                              _layer_block(ffn_w_in, layer),
                              _layer_block(ffn_w_out, layer)],
        out_specs=pl.BlockSpec((tm, d), lambda i: (i, 0)),
        out_shape=jax.ShapeDtypeStruct((t, d), F32),
        compiler_params=_params(("parallel",)),
        name="post_mixer_ffn",
    )(*mix_list, w_out, x2d, gains, ffn_w_in, ffn_w_out)


def _rope_tables(seq):
    half = HEAD_DIM // 2
    inv_freq = ROPE_THETA ** (-jnp.arange(0, HEAD_DIM, 2, dtype=F32) / HEAD_DIM)
    ang = jnp.arange(seq, dtype=F32)[:, None] * inv_freq[None, :]
    reps = LANES // half
    sign = jnp.tile(jnp.concatenate([-jnp.ones((half,), F32), jnp.ones((half,), F32)]), LANES // HEAD_DIM)
    return jnp.tile(jnp.cos(ang), (1, reps)), jnp.tile(jnp.sin(ang), (1, reps)) * sign[None, :]


def kernel(x, norm_gains, att_w_in, att_lambda, att_subln, att_w_out, rec_w_in, rec_lower_bounds,
           rec_gnorm, rec_w_out, ffn_w_in, ffn_w_out):
    b, seq, d = x.shape
    depth = norm_gains.shape[0]
    cos, sin = _rope_tables(seq)
    x2d = x.reshape(b * seq, d)
    for layer in range(depth):
        gain_in = norm_gains[layer, 0:1]
        j = layer // 2
        if layer % 2 == 0:
            lambda_init = 0.8 - 0.6 * math.exp(-0.3 * layer)
            proj = _att_in_proj(x2d, gain_in, att_w_in, j, cos, sin, seq)
            proj = proj.reshape(b, seq, -1)
            oa = _dilated_attention(proj)
            od = _diff_attention(proj, att_lambda[j], att_subln[j][:, None], lambda_init)
            mix = [oa.reshape(b * seq, -1), od.reshape(b * seq, -1)]
            w_out = att_w_out
        else:
            proj, logf = _rec_in_proj(x2d, gain_in, rec_w_in, rec_lower_bounds, j, seq)
            o = _hgrn_recurrence(proj.reshape(b, seq, -1), logf.reshape(b, seq, -1), rec_gnorm[j][None, :])
            mix = [o.reshape(b * seq, -1)]
            w_out = rec_w_out
        x2d = _post_mixer(mix, w_out, j, x2d, norm_gains, ffn_w_in, ffn_w_out, layer, seq)
    return x2d.reshape(b, seq, d)
```

```python
import functools
import math

import jax
import jax.numpy as jnp
import numpy as np
from jax import lax
from jax.experimental import pallas as pl
from jax.experimental.pallas import tpu as pltpu

F32 = jnp.float32
BF16 = jnp.bfloat16

HEAD_DIM = 64
A_HEADS = 8
A_WIDTH = A_HEADS * HEAD_DIM
DILATED_PATTERNS = ((128, 1), (512, 4), (2048, 16))
DIL_BLOCK = 128
DIFF_HEADS = 4
DIFF_QK_WIDTH = 2 * DIFF_HEADS * HEAD_DIM
DIFF_V_DIM = 2 * HEAD_DIM
DIFF_V_WIDTH = DIFF_HEADS * DIFF_V_DIM
ROPE_THETA = 10000.0
HGRN_HEADS = 8
HGRN_KEY_DIM = 128
HGRN_VAL_DIM = 128
NORM_EPS = 1e-6
NEG_INF = -1e30
LOG2_E = math.log2(math.e)

LANES = 128
BF16_ROWS = 16
V7X_VMEM_BYTES = 64 * 1024 * 1024
VMEM_LIMIT = V7X_VMEM_BYTES * 7 // 8

ROW_TILE = 512
IN_ROW_TILE = 1024
COL_CHUNK = 512
DIL_GROUP = 8
DIFF_Q_TILE = 256
DIFF_LOOKAHEAD = 2
HGRN_CHUNK = 64
HGRN_GROUP = 16
HGRN_AHEAD = (2, 2)
HGRN_MIN_HALF_LOG2 = -96.0


def _rms(x, gain):
    ms = jnp.mean(x * x, axis=-1, keepdims=True)
    return x * lax.rsqrt(ms + NORM_EPS) * gain


def _sigmoid(x):
    return 1.0 / (1.0 + jnp.exp(-x))


def _silu(x):
    return x * (0.5 * jnp.tanh(0.5 * x) + 0.5)


def _dot(a, b):
    return jnp.dot(a, b, preferred_element_type=F32)


def _dot_nt(a, b):
    return lax.dot_general(a, b, (((1,), (1,)), ((), ())), preferred_element_type=F32)


def _dot_tn(a, b):
    return lax.dot_general(a, b, (((0,), (0,)), ((), ())), preferred_element_type=F32)


def _params(semantics):
    return pltpu.CompilerParams(dimension_semantics=semantics, vmem_limit_bytes=VMEM_LIMIT)


def _resident(shape):
    return pl.BlockSpec(shape, lambda *_: (0,) * len(shape), pipeline_mode=pl.Buffered(1))


def _layer_block(stack, layer):
    tail = stack.shape[1:]
    return pl.BlockSpec((None,) + tail, lambda *_: (layer,) + (0,) * len(tail), pipeline_mode=pl.Buffered(1))


def _att_in_kernel(x_ref, g_ref, w_ref, cos_ref, sin_ref, o_ref):
    h = _rms(x_ref[...], g_ref[...]).astype(BF16)
    tm = h.shape[0]
    cos = cos_ref[...]
    sin = sin_ref[...]
    lane = lax.broadcasted_iota(jnp.int32, (tm, LANES), 1)
    first_half = (lane % HEAD_DIM) < (HEAD_DIM // 2)
    n_cols = w_ref.shape[1]
    q_scale = HEAD_DIM ** -0.5 * LOG2_E
    rope_hi = 2 * A_WIDTH
    diff_lo = 3 * A_WIDTH
    diff_rope_hi = diff_lo + 2 * DIFF_QK_WIDTH
    for c0 in range(0, n_cols, COL_CHUNK):
        acc = _dot(h, w_ref[:, c0:c0 + COL_CHUNK].astype(BF16))
        for l0 in range(0, COL_CHUNK, LANES):
            col = c0 + l0
            y = acc[:, l0:l0 + LANES]
            if col < rope_hi or diff_lo <= col < diff_rope_hi:
                rot = jnp.where(first_half, pltpu.roll(y, LANES - HEAD_DIM // 2, 1),
                                pltpu.roll(y, HEAD_DIM // 2, 1))
                y = y * cos + rot * sin
                if col < A_WIDTH or diff_lo <= col < diff_lo + DIFF_QK_WIDTH:
                    y = y * q_scale
            o_ref[:, col:col + LANES] = y.astype(BF16)


def _att_in_proj(x2d, gain, w_stack, layer, cos, sin, seq):
    t, d = x2d.shape
    n = w_stack.shape[2]
    tm = min(IN_ROW_TILE, seq)
    tiles_per_seq = seq // tm
    return pl.pallas_call(
        _att_in_kernel,
        grid=(t // tm,),
        in_specs=[pl.BlockSpec((tm, d), lambda i: (i, 0)),
                  _resident((1, d)),
                  _layer_block(w_stack, layer),
                  pl.BlockSpec((tm, LANES), lambda i: (i % tiles_per_seq, 0)),
                  pl.BlockSpec((tm, LANES), lambda i: (i % tiles_per_seq, 0))],
        out_specs=pl.BlockSpec((tm, n), lambda i: (i, 0)),
        out_shape=jax.ShapeDtypeStruct((t, n), BF16),
        compiler_params=_params(("parallel",)),
        name="att_in_proj",
    )(x2d, gain, w_stack, cos, sin)


def _dilated_kernel(q_ref, k_ref, v_ref, o_ref, qf, kf, vf, acc, mx, den):
    seq = q_ref.shape[1]
    blk = DIL_BLOCK
    qf[...] = q_ref[0].astype(F32)
    kf[...] = k_ref[0].astype(F32)
    vf[...] = v_ref[0].astype(F32)

    lane = lax.broadcasted_iota(jnp.int32, (blk, LANES), 1)
    head_a = lane < HEAD_DIM
    qi = lax.broadcasted_iota(jnp.int32, (2 * blk, 2 * blk), 0) % blk
    ci = lax.broadcasted_iota(jnp.int32, (2 * blk, 2 * blk), 1)
    mask_two = (ci >= qi) & (ci <= qi + blk)
    mask_first = (lax.broadcasted_iota(jnp.int32, (blk, 2 * blk), 1) % blk
                  <= lax.broadcasted_iota(jnp.int32, (blk, 2 * blk), 0))

    zero = jnp.zeros((blk, LANES), BF16)
    ones_a = jnp.where(head_a, 1.0, 0.0).astype(BF16)
    ones_b = jnp.where(head_a, 0.0, 1.0).astype(BF16)

    def attend(p, blocks):
        scores = []
        for qb, kc, _, first, _ in blocks:
            if first:
                scores.append(_dot_nt(qb, jnp.concatenate([jnp.where(head_a, kc, zero),
                                                           jnp.where(head_a, zero, kc)], axis=0)))
            else:
                q2 = jnp.concatenate([jnp.where(head_a, qb, zero), jnp.where(head_a, zero, qb)], axis=0)
                scores.append(_dot_nt(q2, kc))
        probs = []
        for s, (_, _, _, first, _) in zip(scores, blocks):
            if first:
                s = jnp.where(mask_first, s, NEG_INF)
                m_a = jnp.max(s[:, :blk], axis=-1, keepdims=True)
                m_b = jnp.max(s[:, blk:], axis=-1, keepdims=True)
                e = jnp.concatenate([jnp.exp2(s[:, :blk] - m_a), jnp.exp2(s[:, blk:] - m_b)], axis=1)
                probs.append((e.astype(BF16), jnp.where(head_a, m_a, m_b)))
            else:
                s = jnp.where(mask_two, s, NEG_INF)
                m = jnp.max(s, axis=-1, keepdims=True)
                probs.append((jnp.exp2(s - m).astype(BF16), m))
        for (e, m), (_, _, vc, first, rows) in zip(probs, blocks):
            if first:
                v_a = jnp.concatenate([jnp.where(head_a, vc, zero), ones_a], axis=1)
                v_b = jnp.concatenate([jnp.where(head_a, zero, vc), ones_b], axis=1)
                o = _dot(e, jnp.concatenate([v_a, v_b], axis=0))
                acc[p, rows, :] = o[:, :LANES]
                mx[p, rows, :] = m
                den[p, rows, :] = o[:, LANES:]
            else:
                o = _dot(e, jnp.concatenate([vc, jnp.ones_like(vc)], axis=1))
                acc[p, rows, :] = jnp.where(head_a, o[:blk, :LANES], o[blk:, :LANES])
                mx[p, rows, :] = jnp.where(head_a, m[:blk], m[blk:])
                den[p, rows, :] = jnp.where(head_a, o[:blk, LANES:], o[blk:, LANES:])

    for p, (window, dil) in enumerate(DILATED_PATTERNS):
        assert window // dil == blk
        length = seq // dil
        nb = length // blk

        if dil == 1:
            for n0 in range(0, nb, DIL_GROUP):
                blocks = []
                for n in range(n0, min(n0 + DIL_GROUP, nb)):
                    keys = slice(max(n - 1, 0) * blk, (n + 1) * blk)
                    blocks.append((q_ref[0, n * blk:(n + 1) * blk, :], k_ref[0, keys, :], v_ref[0, keys, :],
                                   n == 0, pl.ds(n * blk, blk)))
                attend(p, blocks)
            continue

        per_step = max(DIL_GROUP // nb, 1)

        def residues(step, carry, p=p, dil=dil, length=length, nb=nb, per_step=per_step):
            blocks = []
            for j in range(per_step):
                r = step * per_step + j
                qr = qf[pl.ds(r, length, stride=dil), :].astype(BF16)
                kr = kf[pl.ds(r, length, stride=dil), :].astype(BF16)
                vr = vf[pl.ds(r, length, stride=dil), :].astype(BF16)
                for n in range(nb):
                    keys = slice(max(n - 1, 0) * blk, (n + 1) * blk)
                    blocks.append((qr[n * blk:(n + 1) * blk], kr[keys], vr[keys], n == 0,
                                   pl.ds(r + n * blk * dil, blk, stride=dil)))
            attend(p, blocks)
            return carry

        lax.fori_loop(0, dil // per_step, residues, 0)

    m_all = jnp.maximum(jnp.maximum(mx[0], mx[1]), mx[2])
    num = jnp.zeros((seq, LANES), F32)
    tot = jnp.zeros((seq, LANES), F32)
    for p in range(len(DILATED_PATTERNS)):
        w = jnp.exp2(mx[p] - m_all)
        num = num + w * acc[p]
        tot = tot + w * den[p]
    o_ref[0] = (num / tot).astype(BF16)


def _dilated_attention(proj):
    b, seq, _ = proj.shape
    n_pairs = A_WIDTH // LANES
    spec = lambda off: pl.BlockSpec((1, seq, LANES), lambda bi, pi: (bi, 0, off + pi))
    n_pat = len(DILATED_PATTERNS)
    return pl.pallas_call(
        _dilated_kernel,
        grid=(b, n_pairs),
        in_specs=[spec(0), spec(n_pairs), spec(2 * n_pairs)],
        out_specs=pl.BlockSpec((1, seq, LANES), lambda bi, pi: (bi, 0, pi)),
        out_shape=jax.ShapeDtypeStruct((b, seq, A_WIDTH), BF16),
        scratch_shapes=[pltpu.VMEM((seq, LANES), F32)] * 3
        + [pltpu.VMEM((n_pat, seq, LANES), F32)] * 3,
        compiler_params=_params(("parallel", "parallel")),
        name="dilated_attention",
    )(proj, proj, proj)


def _diff_kernel(lam_ref, q_ref, k_ref, v_ref, sg_ref, o_ref, vt_ref, *, lambda_init):
    lp = lam_ref[...]
    l1 = jnp.sum(lp[0:1] * lp[1:2], axis=-1, keepdims=True)
    l2 = jnp.sum(lp[2:3] * lp[3:4], axis=-1, keepdims=True)
    lam = jnp.exp(l1) - jnp.exp(l2) + lambda_init
    seq = q_ref.shape[1]
    tq = min(DIFF_Q_TILE, seq)
    lane = lax.broadcasted_iota(jnp.int32, (tq, LANES), 1)
    sub0 = lane < HEAD_DIM
    zero = jnp.zeros((tq, LANES), BF16)
    gain = sg_ref[...] * (1.0 - lambda_init)

    vt_ref[0:LANES, :] = v_ref[0].astype(F32).T.astype(BF16)
    pad_rows = vt_ref.shape[0] - LANES
    vt_ref[LANES:, :] = jnp.where(lax.broadcasted_iota(jnp.int32, (pad_rows, seq), 0) == 0, 1.0, 0.0).astype(BF16)

    causal = (lax.broadcasted_iota(jnp.int32, (tq, 2 * tq), 0)
              <= lax.broadcasted_iota(jnp.int32, (tq, 2 * tq), 1) % tq)

    def scores(i):
        q = q_ref[0, i * tq:(i + 1) * tq, :]
        q2 = jnp.concatenate([jnp.where(sub0, q, zero), jnp.where(sub0, zero, q)], axis=0)
        past = i * tq
        s_diag = jnp.where(causal, _dot_nt(k_ref[0, past:past + tq, :], q2), NEG_INF)
        s_past = _dot_nt(k_ref[0, 0:past, :], q2) if past else None
        return s_diag, s_past

    n_tiles = seq // tq
    ahead = [scores(i) for i in range(min(DIFF_LOOKAHEAD, n_tiles))]
    for i in range(n_tiles):
        s_diag, s_past = ahead.pop(0)
        if i + DIFF_LOOKAHEAD < n_tiles:
            ahead.append(scores(i + DIFF_LOOKAHEAD))
        past = i * tq
        m = jnp.max(s_diag, axis=0, keepdims=True)
        if past:
            m = jnp.maximum(m, jnp.max(s_past, axis=0, keepdims=True))
        o = _dot(vt_ref[:, past:past + tq], jnp.exp2(s_diag - m).astype(BF16))
        if past:
            o = o + _dot(vt_ref[:, 0:past], jnp.exp2(s_past - m).astype(BF16))
        o = o[:LANES] / o[LANES:LANES + 1]
        w = o[:, :tq] - lam * o[:, tq:]
        ms = jnp.mean(w * w, axis=0, keepdims=True)
        y = w * lax.rsqrt(ms + NORM_EPS) * gain
        o_ref[0, i * tq:(i + 1) * tq, :] = y.T.astype(BF16)


def _diff_attention(proj, lam_params, subln, lambda_init):
    b, seq, _ = proj.shape
    q_off = 3 * A_WIDTH // LANES
    k_off = q_off + DIFF_QK_WIDTH // LANES
    v_off = k_off + DIFF_QK_WIDTH // LANES
    spec = lambda off: pl.BlockSpec((1, seq, LANES), lambda bi, hi: (bi, 0, off + hi))
    return pl.pallas_call(
        functools.partial(_diff_kernel, lambda_init=lambda_init),
        grid=(b, DIFF_HEADS),
        in_specs=[_resident(lam_params.shape), spec(q_off), spec(k_off), spec(v_off),
                  _resident(subln.shape)],
        out_specs=pl.BlockSpec((1, seq, LANES), lambda bi, hi: (bi, 0, hi)),
        out_shape=jax.ShapeDtypeStruct((b, seq, DIFF_V_WIDTH), BF16),
        scratch_shapes=[pltpu.VMEM((DIFF_V_DIM + BF16_ROWS, seq), BF16)],
        compiler_params=_params(("parallel", "parallel")),
        name="diff_attention",
    )(lam_params, proj, proj, proj, subln)


def _rec_in_kernel(x_ref, g_ref, w_ref, lb_ref, o_ref, lf_ref, *, layer):
    h = _rms(x_ref[...], g_ref[...]).astype(BF16)
    lb_raw = lb_ref[...]
    lb_e = jnp.exp(lb_raw - jnp.max(lb_raw, axis=0, keepdims=True))
    lb_p = lb_e / jnp.sum(lb_e, axis=0, keepdims=True)
    lower = jnp.sum(lb_p[0:layer + 1], axis=0, keepdims=True) - lb_p[0:1]
    width = lf_ref.shape[1]
    n_cols = w_ref.shape[1]
    q_scale = HGRN_KEY_DIM ** -0.5
    starts = sorted(range(0, n_cols, COL_CHUNK), key=lambda c: (c // width == 2, c))
    for c0 in starts:
        y = _dot(h, w_ref[:, c0:c0 + COL_CHUNK].astype(BF16))
        group = c0 // width
        if group == 0:
            o_ref[:, c0:c0 + COL_CHUNK] = (_silu(y) * q_scale).astype(BF16)
        elif group == 1:
            f0 = c0 - width
            lo = lower[:, f0:f0 + COL_CHUNK]
            forget = lo + (1.0 - lo) * _sigmoid(y)
            lf_ref[:, f0:f0 + COL_CHUNK] = jnp.log2(forget)
            o_ref[:, c0:c0 + COL_CHUNK] = (1.0 - forget).astype(BF16)
        elif group == 2:
            o_ref[:, c0:c0 + COL_CHUNK] = y.astype(BF16)
        else:
            o_ref[:, c0:c0 + COL_CHUNK] = _silu(y).astype(BF16)


def _rec_in_proj(x2d, gain, w_stack, lower_bounds, layer, seq):
    t, d = x2d.shape
    n = w_stack.shape[2]
    width = lower_bounds.shape[1]
    tm = min(IN_ROW_TILE, seq)
    return pl.pallas_call(
        functools.partial(_rec_in_kernel, layer=layer),
        grid=(t // tm,),
        in_specs=[pl.BlockSpec((tm, d), lambda i: (i, 0)),
                  _resident((1, d)),
                  _layer_block(w_stack, layer),
                  _resident(lower_bounds.shape)],
        out_specs=[pl.BlockSpec((tm, n), lambda i: (i, 0)),
                   pl.BlockSpec((tm, width), lambda i: (i, 0))],
        out_shape=[jax.ShapeDtypeStruct((t, n), BF16),
                   jax.ShapeDtypeStruct((t, width), F32)],
        compiler_params=_params(("parallel",)),
        name="rec_in_proj",
    )(x2d, gain, w_stack, lower_bounds)


def _hgrn_levels():
    return [HGRN_CHUNK >> (i + 1) for i in range(HGRN_CHUNK.bit_length() - 1)]


def _hgrn_tables():
    ch = HGRN_CHUNK
    pair = 2 * ch
    t = np.arange(pair)
    col = np.arange(pair)[None, :]
    same_chunk = (col // ch) == (t[:, None] // ch)

    def prefix(idx):
        return (same_chunk & (col <= idx[:, None])).astype(np.float32)

    last = (t // ch) * ch + ch - 1
    blocks = [prefix(t), prefix(last) - prefix(t)]
    levels = _hgrn_levels()
    for h in levels:
        if h == 1:
            continue
        ref = (t & -(2 * h)) + h - 1
        sign = np.where((t & h) != 0, 1.0, -1.0)[:, None]
        blocks.append(sign * (prefix(t) - prefix(ref)))
    op = np.concatenate(blocks, axis=0)
    op = np.concatenate([op, op], axis=1)

    ti = np.arange(ch)[:, None]
    sj = np.arange(pair)[None, :] & (ch - 1)
    masks = [(ti == sj)]
    for h in levels:
        masks.append((((ti ^ sj) & -(2 * h)) == 0) & ((ti & h) != 0) & ((sj & h) == 0))
    masks.append((((ti ^ sj) & -(ch // 2)) == 0) & (sj <= ti))
    return jnp.asarray(op, BF16), jnp.asarray(np.stack(masks), F32)


def _hgrn_kernel(op_ref, mask_ref, q_ref, k_ref, v_ref, gate_ref, lf_ref, gn_ref, o_ref, st_ref):
    seq = q_ref.shape[1]
    ch = HGRN_CHUNK
    pair = 2 * ch
    st_ref[...] = jnp.zeros_like(st_ref)
    gain = gn_ref[...]
    halves = _hgrn_levels()
    first_chunk = lax.broadcasted_iota(jnp.int32, (ch, pair), 1) < ch
    row_id = lax.broadcasted_iota(jnp.int32, (pair, LANES), 0)
    odd_row = (row_id & 1) == 1
    first_half = (row_id & (ch // 2)) == 0
    zeros_half = jnp.zeros((ch, LANES), BF16)

    def side_by_side(x):
        return jnp.concatenate([x[:ch], x[ch:]], axis=1)

    def block_diag(x):
        return jnp.concatenate([jnp.concatenate([x[:ch], zeros_half], axis=1),
                                jnp.concatenate([zeros_half, x[ch:]], axis=1)], axis=0)

    group = min(HGRN_GROUP, seq // pair)

    def pair_rows(gi, c):
        return pl.ds(pl.multiple_of(gi * (group * pair), pair) + c * pair, pair)

    def chunk_bcast(b, row):
        return jnp.concatenate([jnp.broadcast_to(b[c0 + row:c0 + row + 1, :], (ch, LANES))
                                for c0 in (0, ch)], axis=0)

    def scores_any_decay(qb, kb, cum, g):
        attn = _dot_nt(side_by_side(qb), block_diag(kb)) * mask_ref[0]
        for li, h in enumerate(halves):
            if h == 1:
                e = jnp.where(odd_row, jnp.exp2(g), 1.0).astype(BF16)
            else:
                e = jnp.exp2(cum[(2 + li) * pair:(3 + li) * pair]).astype(BF16)
            attn = attn + _dot_nt(side_by_side(qb * e), block_diag(kb * e)) * mask_ref[1 + li]
        b = cum[:pair]
        return attn, qb * jnp.exp2(b).astype(BF16), kb * jnp.exp2(cum[pair:2 * pair]).astype(BF16)

    def scores_bounded_decay(qb, kb, b):
        mid = ch // 2
        d = b - chunk_bcast(b, mid - 1)
        e = jnp.exp2(-jnp.abs(d)).astype(BF16)
        q_top = qb * e
        attn = _dot_nt(side_by_side(q_top), block_diag(kb * e)) * mask_ref[1]
        q_dec = qb * jnp.exp2(b).astype(BF16)
        q_loc = jnp.where(first_half, q_dec, q_top)
        k_loc = kb * jnp.exp2(jnp.where(first_half, -b, -d)).astype(BF16)
        attn = attn + _dot_nt(side_by_side(q_loc), block_diag(k_loc)) * mask_ref[len(halves) + 1]
        k_dec = kb * jnp.exp2(chunk_bcast(b, ch - 1) - b).astype(BF16)
        return attn, q_dec, k_dec

    def decay_exponents(rows, bounded):
        op = op_ref[:pair, :] if bounded else op_ref[...]
        g = lf_ref[0, rows, :]
        g_hi = g.astype(BF16)
        g_lo = (g - g_hi.astype(F32)).astype(BF16)
        return g, _dot(op, jnp.concatenate([g_hi, g_lo], axis=0))

    def pair_scores(rows, g, cum, bounded):
        qb = q_ref[0, rows, :]
        kb = k_ref[0, rows, :]
        b = cum[:pair]
        if bounded:
            attn, q_dec, k_dec = scores_bounded_decay(qb, kb, b)
        else:
            attn, q_dec, k_dec = scores_any_decay(qb, kb, cum, g)
        cross = _dot_nt(q_dec[ch:], k_dec)
        lhs = jnp.concatenate([jnp.where(first_chunk, attn, 0.0),
                               jnp.where(first_chunk, cross, attn)], axis=0).astype(BF16)
        tot0 = jnp.exp2(b[ch - 1:ch, :])
        tot1 = jnp.exp2(b[pair - 1:pair, :])
        q_in = jnp.concatenate([q_dec[:ch], q_dec[ch:] * tot0.astype(BF16)], axis=0)
        k_up = jnp.concatenate([k_dec[:ch] * tot1.astype(BF16), k_dec[ch:]], axis=0)
        return lhs, q_in, k_up, tot0 * tot1

    def step(gi, carry, bounded):
        ahead_exp, ahead_sc = HGRN_AHEAD
        exps, scs = {}, {}
        st = st_ref[...]
        for tick in range(group + ahead_exp + ahead_sc):
            if tick < group:
                exps[tick] = decay_exponents(pair_rows(gi, tick), bounded)
            c = tick - ahead_exp
            if 0 <= c < group:
                scs[c] = pair_scores(pair_rows(gi, c), *exps.pop(c), bounded)
            c = tick - ahead_exp - ahead_sc
            if 0 <= c < group:
                rows = pair_rows(gi, c)
                v = v_ref[0, rows, :]
                lhs, q_in, k_up, total = scs.pop(c)
                o = _dot_nt(q_in, st.astype(BF16)) + _dot(lhs, v)
                st = st * total + _dot_tn(v, k_up)
                y = _rms(o, gain) * gate_ref[0, rows, :].astype(F32)
                o_ref[0, rows, :] = y.astype(BF16)
        st_ref[...] = st
        return carry

    half_rows = ch // 2
    half_tot = jnp.sum(lf_ref[0].reshape(seq // half_rows, half_rows, LANES), axis=1)
    worst = jnp.min(jnp.min(half_tot, axis=0, keepdims=True), axis=1, keepdims=True)
    bounded_ok = worst[0, 0] >= HGRN_MIN_HALF_LOG2
    n_steps = seq // (group * pair)

    @pl.when(bounded_ok)
    def _():
        lax.fori_loop(0, n_steps, functools.partial(step, bounded=True), 0)

    @pl.when(jnp.logical_not(bounded_ok))
    def _():
        lax.fori_loop(0, n_steps, functools.partial(step, bounded=False), 0)


def _hgrn_recurrence(proj, logf, gnorm):
    b, seq, _ = proj.shape
    h = HGRN_HEADS
    spec = lambda off: pl.BlockSpec((1, seq, LANES), lambda bi, hi: (bi, 0, off + hi))
    op, masks = _hgrn_tables()
    return pl.pallas_call(
        _hgrn_kernel,
        grid=(b, h),
        in_specs=[_resident(op.shape), _resident(masks.shape),
                  spec(0), spec(h), spec(2 * h), spec(3 * h), spec(0), _resident(gnorm.shape)],
        out_specs=pl.BlockSpec((1, seq, LANES), lambda bi, hi: (bi, 0, hi)),
        out_shape=jax.ShapeDtypeStruct((b, seq, h * HGRN_VAL_DIM), BF16),
        scratch_shapes=[pltpu.VMEM((HGRN_VAL_DIM, HGRN_KEY_DIM), F32)],
        compiler_params=_params(("parallel", "parallel")),
        name="hgrn_recurrence",
    )(op, masks, proj, proj, proj, proj, logf, gnorm)


def _post_kernel(*refs, n_mix, hidden, chunks, layer):
    mix_refs = refs[:n_mix]
    wo_ref, x_ref, g_ref, win_hbm, wout_hbm, o_ref, win_ref, wout_ref, sems = refs[n_mix:]

    def copies(c):
        lo, hi = chunks[c]
        return (pltpu.make_async_copy(win_hbm.at[layer, :, lo:hi], win_ref.at[:, lo:hi], sems.at[0, c]),
                pltpu.make_async_copy(win_hbm.at[layer, :, hidden + lo:hidden + hi],
                                      win_ref.at[:, hidden + lo:hidden + hi], sems.at[1, c]),
                pltpu.make_async_copy(wout_hbm.at[layer, lo:hi, :], wout_ref.at[lo:hi, :], sems.at[2, c]))

    def body(first):
        if first:
            for c in range(len(chunks)):
                for cp in copies(c):
                    cp.start()
        gains = g_ref[...]
        mixed = jnp.concatenate([r[...] for r in mix_refs], axis=-1) if n_mix > 1 else mix_refs[0][...]
        x1 = x_ref[...] + _rms(_dot(mixed, wo_ref[...].astype(BF16)), gains[1:2])
        h = _rms(x1, gains[2:3]).astype(BF16)
        acc = jnp.zeros(x1.shape, F32)
        for c, (lo, hi) in enumerate(chunks):
            if first:
                gate_cp, up_cp, out_cp = copies(c)
                gate_cp.wait()
                up_cp.wait()
            gate = _dot(h, win_ref[:, lo:hi].astype(BF16))
            up = _dot(h, win_ref[:, hidden + lo:hidden + hi].astype(BF16))
            act = (_silu(gate) * up).astype(BF16)
            if first:
                out_cp.wait()
            acc = acc + _dot(act, wout_ref[lo:hi, :].astype(BF16))
        o_ref[...] = x1 + _rms(acc, gains[3:4])

    @pl.when(pl.program_id(0) == 0)
    def _():
        body(True)

    @pl.when(pl.program_id(0) != 0)
    def _():
        body(False)


def _post_mixer(mix_list, w_out, mixer_layer, x2d, gains, ffn_w_in, ffn_w_out, layer, seq):
    t, d = x2d.shape
    hidden = ffn_w_out.shape[1]
    tm = min(ROW_TILE, seq)
    chunks = tuple((lo, min(lo + COL_CHUNK, hidden)) for lo in range(0, hidden, COL_CHUNK))
    mix_specs = [pl.BlockSpec((tm, m.shape[1]), lambda i: (i, 0)) for m in mix_list]
    return pl.pallas_call(
        functools.partial(_post_kernel, n_mix=len(mix_list), hidden=hidden, chunks=chunks, layer=layer),
        grid=(t // tm,),
        in_specs=mix_specs + [_layer_block(w_out, mixer_layer),
                              pl.BlockSpec((tm, d), lambda i: (i, 0)),
                              _layer_block(gains, layer),
                              pl.BlockSpec(memory_space=pl.ANY),
                              pl.BlockSpec(memory_space=pl.ANY)],
        out_specs=pl.BlockSpec((tm, d), lambda i: (i, 0)),
        out_shape=jax.ShapeDtypeStruct((t, d), F32),
        scratch_shapes=[pltpu.VMEM(ffn_w_in.shape[1:], ffn_w_in.dtype),
                        pltpu.VMEM(ffn_w_out.shape[1:], ffn_w_out.dtype),
                        pltpu.SemaphoreType.DMA((3, len(chunks)))],
        compiler_params=_params(("arbitrary",)),
        name="post_mixer_ffn",
    )(*mix_list, w_out, x2d, gains, ffn_w_in, ffn_w_out)


def _rope_tables(seq):
    half = HEAD_DIM // 2
    inv_freq = ROPE_THETA ** (-jnp.arange(0, HEAD_DIM, 2, dtype=F32) / HEAD_DIM)
    ang = jnp.arange(seq, dtype=F32)[:, None] * inv_freq[None, :]
    reps = LANES // half
    sign = jnp.tile(jnp.concatenate([-jnp.ones((half,), F32), jnp.ones((half,), F32)]), LANES // HEAD_DIM)
    return jnp.tile(jnp.cos(ang), (1, reps)), jnp.tile(jnp.sin(ang), (1, reps)) * sign[None, :]


def kernel(x, norm_gains, att_w_in, att_lambda, att_subln, att_w_out, rec_w_in, rec_lower_bounds,
           rec_gnorm, rec_w_out, ffn_w_in, ffn_w_out):
    b, seq, d = x.shape
    depth = norm_gains.shape[0]
    cos, sin = _rope_tables(seq)
    x2d = x.reshape(b * seq, d)
    for layer in range(depth):
        gain_in = norm_gains[layer, 0:1]
        j = layer // 2
        if layer % 2 == 0:
            lambda_init = 0.8 - 0.6 * math.exp(-0.3 * layer)
            proj = _att_in_proj(x2d, gain_in, att_w_in, j, cos, sin, seq)
            proj = proj.reshape(b, seq, -1)
            oa = _dilated_attention(proj)
            od = _diff_attention(proj, att_lambda[j], att_subln[j][:, None], lambda_init)
            mix = [oa.reshape(b * seq, -1), od.reshape(b * seq, -1)]
            w_out = att_w_out
        else:
            proj, logf = _rec_in_proj(x2d, gain_in, rec_w_in, rec_lower_bounds, j, seq)
            o = _hgrn_recurrence(proj.reshape(b, seq, -1), logf.reshape(b, seq, -1), rec_gnorm[j][None, :])
            mix = [o.reshape(b * seq, -1)]
            w_out = rec_w_out
        x2d = _post_mixer(mix, w_out, j, x2d, norm_gains, ffn_w_in, ffn_w_out, layer, seq)
    return x2d.reshape(b, seq, d)
```
